```python
import math
import jax, jax.numpy as jnp
from jax import lax
import numpy as np


D_MODEL = 1024
BATCH = 32
SEQ = 2048
DEPTH = 2

HEAD_DIM = D_MODEL // 16
MIX_WIDTH = D_MODEL // 2
FOX_HEADS = MIX_WIDTH // HEAD_DIM
SWA_Q_HEADS = MIX_WIDTH // HEAD_DIM
SWA_KV_HEADS = 2
SWA_WINDOW = 128
MLSTM_HEADS = 4
MLSTM_HEAD_DIM = MIX_WIDTH // MLSTM_HEADS
CONV_WIDTH = 4
D_FF = 4 * D_MODEL
N_BRANCHES = 3
BLOCK = 128
ROPE_THETA = 10000.0
EPS = 1e-6

IN_SPLITS = (
    MIX_WIDTH, MIX_WIDTH, MIX_WIDTH, FOX_HEADS,
    SWA_Q_HEADS * HEAD_DIM, SWA_KV_HEADS * HEAD_DIM, SWA_KV_HEADS * HEAD_DIM,
    MIX_WIDTH, MIX_WIDTH, MIX_WIDTH, MLSTM_HEADS, MLSTM_HEADS, MIX_WIDTH,
    N_BRANCHES * D_MODEL,
)
IN_WIDTH = sum(IN_SPLITS)

kernel_name = 'hybrid_fox_swa_mlstm_block'


def _rmsnorm(x, g):
    xf = x.astype(jnp.float32)
    y = xf * lax.rsqrt(jnp.mean(xf * xf, axis=-1, keepdims=True) + EPS)
    return (y * g.astype(jnp.float32)).astype(x.dtype)


def _rope_tables(seq, dim):
    inv = ROPE_THETA ** (-jnp.arange(0, dim, 2, dtype=jnp.float32) / dim)
    ang = jnp.arange(seq, dtype=jnp.float32)[:, None] * inv[None, :]
    return jnp.cos(ang), jnp.sin(ang)


def _rope(x, cos, sin):
    xf = x.astype(jnp.float32)
    x1, x2 = jnp.split(xf, 2, axis=-1)
    c = cos[None, :, None, :]
    s = sin[None, :, None, :]
    return jnp.concatenate([x1 * c - x2 * s, x1 * s + x2 * c], axis=-1).astype(x.dtype)


def _fox_attention(q, k, v, f_logit):
    B, S, H, Dh = q.shape
    nb = S // BLOCK
    scale = Dh ** -0.5
    c = jnp.cumsum(jax.nn.log_sigmoid(f_logit.astype(jnp.float32)), axis=1).transpose(0, 2, 1)
    qb = q.reshape(B, nb, BLOCK, H, Dh).transpose(1, 0, 2, 3, 4)
    cb = c.reshape(B, H, nb, BLOCK).transpose(2, 0, 1, 3)
    k_pos = jnp.arange(S)

    def one_block(args):
        i, qi, ci = args
        logits = jnp.einsum('bqhd,bshd->bhqs', qi, k, preferred_element_type=jnp.float32) * scale
        logits = logits + ci[..., :, None] - c[:, :, None, :]
        q_pos = i * BLOCK + jnp.arange(BLOCK)
        mask = k_pos[None, :] <= q_pos[:, None]
        p = jax.nn.softmax(jnp.where(mask, logits, -jnp.inf), axis=-1)
        return jnp.einsum('bhqs,bshd->bqhd', p.astype(v.dtype), v)

    out = lax.map(one_block, (jnp.arange(nb), qb, cb))
    return out.transpose(1, 0, 2, 3, 4).reshape(B, S, H * Dh)


def _swa_sink_attention(q, k, v, sinks):
    B, S, Hq, Dh = q.shape
    Hkv = k.shape[2]
    G = Hq // Hkv
    nb = S // BLOCK
    scale = Dh ** -0.5
    qb = q.reshape(B, nb, BLOCK, Hkv, G, Dh)

    def frame(t):
        tb = t.reshape(B, nb, BLOCK, Hkv, Dh)
        prev = jnp.pad(tb, ((0, 0), (1, 0), (0, 0), (0, 0), (0, 0)))[:, :-1]
        return jnp.concatenate([prev, tb], axis=2)

    kf, vf = frame(k), frame(v)
    logits = jnp.einsum('bnqhgd,bnkhd->bnhgqk', qb, kf, preferred_element_type=jnp.float32) * scale
    q_pos = jnp.arange(nb)[:, None, None] * BLOCK + jnp.arange(BLOCK)[None, :, None]
    k_pos = jnp.arange(nb)[:, None, None] * BLOCK - BLOCK + jnp.arange(2 * BLOCK)[None, None, :]
    mask = (k_pos <= q_pos) & (k_pos > q_pos - SWA_WINDOW) & (k_pos >= 0)
    logits = jnp.where(mask[None, :, None, None], logits, -jnp.inf)
    sink = jnp.broadcast_to(sinks.astype(jnp.float32).reshape(1, 1, Hkv, G, 1, 1), logits.shape[:-1] + (1,))
    p = jax.nn.softmax(jnp.concatenate([logits, sink], axis=-1), axis=-1)[..., :-1]
    out = jnp.einsum('bnhgqk,bnkhd->bnqhgd', p.astype(v.dtype), vf)
    return out.reshape(B, S, Hq * Dh)


def _causal_conv_silu(x, w, b):
    S = x.shape[1]
    W = w.shape[0]
    xp = jnp.pad(x, ((0, 0), (W - 1, 0), (0, 0)))
    y = b
    for j in range(W):
        y = y + xp[:, j:j + S] * w[j]
    return jax.nn.silu(y)


def _mlstm_chunkwise(q, k, v, i_logit, f_logit):
    B, S, H, Dh = q.shape
    L = BLOCK
    nc = S // L
    f32 = jnp.float32

    def chunks(t):
        return t.astype(f32).reshape(B, nc, L, H, -1).transpose(1, 0, 3, 2, 4)

    qc = chunks(q)
    kc = chunks(k) * (Dh ** -0.5)
    vc = chunks(v)
    ic = i_logit.astype(f32).reshape(B, nc, L, H).transpose(1, 0, 3, 2)
    bc = jnp.cumsum(jax.nn.log_sigmoid(f_logit.astype(f32)).reshape(B, nc, L, H).transpose(1, 0, 3, 2), axis=-1)
    causal = jnp.tril(jnp.ones((L, L), dtype=bool))

    def step(carry, xs):
        C, n, m = carry
        qi, ki, vi, ii, bi = xs
        a = bi + m[..., None]
        D = jnp.where(causal, bi[..., :, None] - bi[..., None, :] + ii[..., None, :], -jnp.inf)
        mt = jnp.maximum(a, jnp.max(D, axis=-1))
        w_inter = jnp.exp(a - mt)
        s = jnp.exp(D - mt[..., None]) * jnp.einsum('bhtd,bhsd->bhts', qi, ki)
        num = w_inter[..., None] * jnp.einsum('bhtd,bhde->bhte', qi, C) + jnp.einsum('bhts,bhse->bhte', s, vi)
        den = w_inter * jnp.einsum('bhtd,bhd->bht', qi, n) + jnp.sum(s, axis=-1)
        h = num / jnp.maximum(jnp.abs(den), jnp.exp(-mt))[..., None]
        bL = bi[..., -1]
        g = bL[..., None] - bi + ii
        m_new = jnp.maximum(bL + m, jnp.max(g, axis=-1))
        decay = jnp.exp(bL + m - m_new)
        wk = jnp.exp(g - m_new[..., None])
        C_new = decay[..., None, None] * C + jnp.einsum('bhs,bhsd,bhse->bhde', wk, ki, vi)
        n_new = decay[..., None] * n + jnp.einsum('bhs,bhsd->bhd', wk, ki)
        return (C_new, n_new, m_new), h

    init = (jnp.zeros((B, H, Dh, Dh), f32), jnp.zeros((B, H, Dh), f32), jnp.zeros((B, H), f32))
    _, hs = lax.scan(step, init, (qc, kc, vc, ic, bc))
    return hs.transpose(1, 0, 3, 2, 4).reshape(B, S, H, Dh)


def _hybrid_layer(x, cos, sin, norm_mix, w_in, fox_f_bias, fox_q_norm, fox_k_norm,
                  swa_q_norm, swa_k_norm, swa_sinks, conv_w, conv_b, mlstm_i_bias,
                  mlstm_f_bias, mlstm_out_norm, w_branch, w_out, norm_mlp, w_up, w_down):
    B, S, _ = x.shape
    h = _rmsnorm(x, norm_mix)
    u = jnp.einsum('bsd,de->bse', h, w_in)
    offsets = np.cumsum(IN_SPLITS)[:-1].tolist()
    fq, fk, fv, ff, sq, sk, sv, mq, mk, mv, mi, mf, mo, gl = jnp.split(u, offsets, axis=-1)

    def heads(t, n_heads):
        return t.reshape(B, S, n_heads, -1)

    y_fox = _fox_attention(_rmsnorm(heads(fq, FOX_HEADS), fox_q_norm),
                           _rmsnorm(heads(fk, FOX_HEADS), fox_k_norm),
                           heads(fv, FOX_HEADS), ff + fox_f_bias)
    sq = _rope(_rmsnorm(heads(sq, SWA_Q_HEADS), swa_q_norm), cos, sin)
    sk = _rope(_rmsnorm(heads(sk, SWA_KV_HEADS), swa_k_norm), cos, sin)
    y_swa = _swa_sink_attention(sq, sk, heads(sv, SWA_KV_HEADS), swa_sinks)
    qk = _causal_conv_silu(jnp.concatenate([mq, mk], axis=-1), conv_w, conv_b)
    mq, mk = jnp.split(qk, 2, axis=-1)
    hm = _mlstm_chunkwise(heads(mq, MLSTM_HEADS), heads(mk, MLSTM_HEADS), heads(mv, MLSTM_HEADS),
                          mi + mlstm_i_bias, mf + mlstm_f_bias)
    hm = _rmsnorm(hm, mlstm_out_norm.reshape(MLSTM_HEADS, MLSTM_HEAD_DIM)).reshape(B, S, MIX_WIDTH)
    y_mlstm = (hm * jax.nn.sigmoid(mo.astype(jnp.float32))).astype(x.dtype)

    gates = jax.nn.sigmoid(gl.reshape(B, S, N_BRANCHES, D_MODEL))
    merged = (gates[:, :, 0] * jnp.einsum('bsc,cd->bsd', y_fox, w_branch[0])
              + gates[:, :, 1] * jnp.einsum('bsc,cd->bsd', y_swa, w_branch[1])
              + gates[:, :, 2] * jnp.einsum('bsc,cd->bsd', y_mlstm, w_branch[2]))
    x = x + jnp.einsum('bsd,de->bse', merged, w_out)

    h2 = _rmsnorm(x, norm_mlp)
    act = jnp.square(jax.nn.relu(jnp.einsum('bsd,df->bsf', h2, w_up)))
    return x + jnp.einsum('bsf,fd->bsd', act, w_down)


def setup_inputs(seed: int = 0) -> dict:
    key = jax.random.key(seed)
    ks = jax.random.split(key, 20)
    f32 = jnp.float32
    nrm = lambda k, shape, s: jax.random.normal(k, shape, f32) * s
    gain = lambda k, shape: 1.0 + 0.05 * jax.random.normal(k, shape, f32)
    return {
        'x': nrm(ks[0], (BATCH, SEQ, D_MODEL), 1.0),
        'norm_mix': gain(ks[1], (DEPTH, D_MODEL)),
        'w_in': nrm(ks[2], (DEPTH, D_MODEL, IN_WIDTH), D_MODEL ** -0.5),
        'fox_f_bias': 3.0 + 0.5 * jax.random.normal(ks[3], (DEPTH, FOX_HEADS), f32),
        'fox_q_norm': gain(ks[4], (DEPTH, HEAD_DIM)),
        'fox_k_norm': gain(ks[5], (DEPTH, HEAD_DIM)),
        'swa_q_norm': gain(ks[6], (DEPTH, HEAD_DIM)),
        'swa_k_norm': gain(ks[7], (DEPTH, HEAD_DIM)),
        'swa_sinks': nrm(ks[8], (DEPTH, SWA_Q_HEADS), 0.5),
        'conv_w': nrm(ks[9], (DEPTH, CONV_WIDTH, 2 * MIX_WIDTH), CONV_WIDTH ** -0.5),
        'conv_b': nrm(ks[10], (DEPTH, 2 * MIX_WIDTH), 0.02),
        'mlstm_i_bias': nrm(ks[11], (DEPTH, MLSTM_HEADS), 0.1),
        'mlstm_f_bias': 3.0 + 0.5 * jax.random.normal(ks[12], (DEPTH, MLSTM_HEADS), f32),
        'mlstm_out_norm': gain(ks[13], (DEPTH, MIX_WIDTH)),
        'w_branch': nrm(ks[14], (DEPTH, N_BRANCHES, MIX_WIDTH, D_MODEL), MIX_WIDTH ** -0.5),
        'w_out': nrm(ks[15], (DEPTH, D_MODEL, D_MODEL), D_MODEL ** -0.5),
        'norm_mlp': gain(ks[16], (DEPTH, D_MODEL)),
        'w_up': nrm(ks[17], (DEPTH, D_MODEL, D_FF), D_MODEL ** -0.5),
        'w_down': nrm(ks[18], (DEPTH, D_FF, D_MODEL), D_FF ** -0.5),
    }


def reference(x, norm_mix, w_in, fox_f_bias, fox_q_norm, fox_k_norm, swa_q_norm, swa_k_norm,
              swa_sinks, conv_w, conv_b, mlstm_i_bias, mlstm_f_bias, mlstm_out_norm, w_branch,
              w_out, norm_mlp, w_up, w_down):
    S = x.shape[1]
    cos, sin = _rope_tables(S, HEAD_DIM)
    for l in range(DEPTH):
        x = _hybrid_layer(x, cos, sin, norm_mix[l], w_in[l], fox_f_bias[l], fox_q_norm[l],
                          fox_k_norm[l], swa_q_norm[l], swa_k_norm[l], swa_sinks[l], conv_w[l],
                          conv_b[l], mlstm_i_bias[l], mlstm_f_bias[l], mlstm_out_norm[l],
                          w_branch[l], w_out[l], norm_mlp[l], w_up[l], w_down[l])
    return x
```

```python
import functools

import jax
import jax.numpy as jnp
import numpy as np
from jax import lax
from jax.experimental import pallas as pl
from jax.experimental.pallas import tpu as pltpu

F32 = jnp.float32
BF16 = jnp.bfloat16

D_MODEL = 1024
HEAD_DIM = 64
MIX_WIDTH = 512
FOX_HEADS = 8
SWA_Q_HEADS = 8
SWA_KV_HEADS = 2
SWA_WINDOW = 128
MLSTM_HEADS = 4
MLSTM_HEAD_DIM = 128
CONV_WIDTH = 4
D_FF = 4 * D_MODEL
N_BRANCHES = 3
BLOCK = 128
ROPE_THETA = 10000.0
EPS = 1e-6
NEG = -1e30

LANES = 128
VMEM_LIMIT_BYTES = 56 * 1024 * 1024

_FK, _SQ, _SK, _SV, _MQ, _MK, _MV, _MO, _GT, _NTOK = 0, 512, 1024, 1280, 1536, 2048, 2560, 3072, 3584, 3712
_GI, _GF = 32, 36

IN_SPLITS = (512, 512, 512, 8, 512, 128, 128, 512, 512, 512, 4, 4, 512, 3 * D_MODEL)


def _nt_dot(a, b):
    return lax.dot_general(a, b, (((1,), (1,)), ((), ())), preferred_element_type=F32)


def _tn_dot(a, b):
    return lax.dot_general(a, b, (((0,), (0,)), ((), ())), preferred_element_type=F32)


def _sigmoid(x):
    return 1.0 / (1.0 + jnp.exp(-x))


def _log_sigmoid(x):
    return jnp.minimum(x, 0.0) - jnp.log(1.0 + jnp.exp(-jnp.abs(x)))


def _rmsnorm_rows(x, gain_row):
    ms = jnp.mean(x * x, axis=-1, keepdims=True)
    return x * lax.rsqrt(ms + EPS) * gain_row


def _in_proj_kernel(x_ref, g_ref, wt_ref, wT_ref, fkg_ref, sqg_ref, skg_ref, gm_ref, cos_ref, sin_ref,
                    gbias_ref, ltri_ref,
                    fqT_ref, fvT_ref, fk_ref, sq_ref, sk_ref, sv_ref, mq_ref, mk_ref, mv_ref, og_ref,
                    gtok_ref, gT_ref, carry_ref, *, tm, tkv):
    s = pl.program_id(1)
    h = _rmsnorm_rows(x_ref[0], g_ref[...]).astype(BF16)

    def proj(lo, hi):
        return jnp.dot(h, wt_ref[:, lo:hi], preferred_element_type=F32)

    def head_ms(u):
        w = u.shape[1]
        return jnp.dot((u * u).astype(BF16), gm_ref[:w, :w], preferred_element_type=F32)

    def rope(u):
        lane = lax.broadcasted_iota(jnp.int32, (tm, LANES), 1)
        first = (lane % HEAD_DIM) < (HEAD_DIM // 2)
        c, sn = cos_ref[...], sin_ref[...]
        outs = []
        for j in range(u.shape[1] // LANES):
            uj = u[:, j * LANES:(j + 1) * LANES]
            partner = jnp.where(first, pltpu.roll(uj, LANES - HEAD_DIM // 2, 1), pltpu.roll(uj, HEAD_DIM // 2, 1))
            outs.append(uj * c + partner * sn)
        return jnp.concatenate(outs, axis=1)

    qT = _nt_dot(wT_ref[0:MIX_WIDTH, :], h)
    msT = jnp.dot(gm_ref[...], (qT * qT).astype(BF16), preferred_element_type=F32)
    fqT_ref[0] = (qT * lax.rsqrt(msT + EPS)).astype(BF16)
    vT = _nt_dot(wT_ref[MIX_WIDTH:2 * MIX_WIDTH, :], h).astype(BF16)
    for c in range(tm // tkv):
        fvT_ref[0, c] = vT[:, c * tkv:(c + 1) * tkv]

    u = proj(_FK, _SQ)
    fk_ref[0] = (u * lax.rsqrt(head_ms(u) + EPS) * fkg_ref[...]).astype(BF16)

    u = proj(_SQ, _SK)
    sq_ref[0] = rope(u * lax.rsqrt(head_ms(u) + EPS) * sqg_ref[...]).astype(BF16)
    u = proj(_SK, _SV)
    sk_ref[0] = rope(u * lax.rsqrt(head_ms(u) + EPS) * skg_ref[...]).astype(BF16)
    sv_ref[0] = proj(_SV, _MQ).astype(BF16)

    mq_ref[0] = proj(_MQ, _MK).astype(BF16)
    mk_ref[0] = proj(_MK, _MV).astype(BF16)
    mv_ref[0] = proj(_MV, _MO).astype(BF16)
    og_ref[0] = _sigmoid(proj(_MO, _GT)).astype(BF16)

    @pl.when(s == 0)
    def _():
        carry_ref[...] = jnp.zeros_like(carry_ref)

    g = proj(_GT, _NTOK) + gbias_ref[...]
    col = lax.broadcasted_iota(jnp.int32, (BLOCK, LANES), 1)
    is_i = (col >= _GI) & (col < _GF)
    is_fox = col < _GI
    ltri = ltri_ref[...]
    for c in range(tm // BLOCK):
        gc = g[c * BLOCK:(c + 1) * BLOCK]
        val = _log_sigmoid(gc)
        hi = val.astype(BF16)
        r1 = val - hi.astype(F32)
        mid = r1.astype(BF16)
        lo = (r1 - mid.astype(F32)).astype(BF16)
        cs = (jnp.dot(ltri, hi, preferred_element_type=F32) + jnp.dot(ltri, mid, preferred_element_type=F32)
              + jnp.dot(ltri, lo, preferred_element_type=F32))
        cs = cs + jnp.where(is_fox, carry_ref[...], 0.0)
        out_c = jnp.where(is_i, gc, cs)
        carry_ref[...] = out_c[BLOCK - 1:BLOCK, :]
        gtok_ref[0, c * BLOCK:(c + 1) * BLOCK, :] = out_c
        gT_ref[0, c] = out_c.T


def _in_proj(x, gain, wt, wT, fkg, sqg, skg, gm, cosf, sinf, gbias, ltri, *, tm, tkv):
    B, S, D = x.shape
    nc = S // BLOCK
    const = lambda shape: pl.BlockSpec(shape, lambda b, s: (0,) * len(shape))
    tok = lambda w: pl.BlockSpec((1, tm, w), lambda b, s: (b, s, 0))
    out_shape = (
        jax.ShapeDtypeStruct((B, MIX_WIDTH, S), BF16),
        jax.ShapeDtypeStruct((B, S // tkv, MIX_WIDTH, tkv), BF16),
        jax.ShapeDtypeStruct((B, S, 512), BF16),
        jax.ShapeDtypeStruct((B, S, 512), BF16),
        jax.ShapeDtypeStruct((B, S, 256), BF16),
        jax.ShapeDtypeStruct((B, S, 256), BF16),
        jax.ShapeDtypeStruct((B, S, 512), BF16),
        jax.ShapeDtypeStruct((B, S, 512), BF16),
        jax.ShapeDtypeStruct((B, S, 512), BF16),
        jax.ShapeDtypeStruct((B, S, 512), BF16),
        jax.ShapeDtypeStruct((B, S, LANES), F32),
        jax.ShapeDtypeStruct((B, nc, LANES, BLOCK), F32),
    )
    out_specs = (
        pl.BlockSpec((1, MIX_WIDTH, tm), lambda b, s: (b, 0, s)),
        pl.BlockSpec((1, tm // tkv, MIX_WIDTH, tkv), lambda b, s: (b, s, 0, 0)),
        tok(512), tok(512), tok(256), tok(256), tok(512), tok(512), tok(512), tok(512), tok(LANES),
        pl.BlockSpec((1, tm // BLOCK, LANES, BLOCK), lambda b, s: (b, s, 0, 0)),
    )
    in_specs = [
        pl.BlockSpec((1, tm, D), lambda b, s: (b, s, 0)),
        const((1, D)), const(wt.shape), const(wT.shape), const((1, 512)), const((1, 512)), const((1, 256)),
        const((512, 512)),
        pl.BlockSpec((tm, LANES), lambda b, s: (s, 0)), pl.BlockSpec((tm, LANES), lambda b, s: (s, 0)),
        const((1, LANES)), const((BLOCK, BLOCK)),
    ]
    return pl.pallas_call(
        functools.partial(_in_proj_kernel, tm=tm, tkv=tkv),
        grid=(B, S // tm), in_specs=in_specs, out_specs=out_specs, out_shape=out_shape,
        scratch_shapes=[pltpu.VMEM((1, LANES), F32)],
        compiler_params=pltpu.CompilerParams(dimension_semantics=("parallel", "arbitrary"),
                                             vmem_limit_bytes=VMEM_LIMIT_BYTES),
        name="in_proj",
    )(x, gain, wt, wT, fkg, sqg, skg, gm, cosf, sinf, gbias, ltri)


def _fox_kernel(qT_ref, k_ref, vT_ref, g_ref, o_ref, nb_ref, *, tq):
    p = pl.program_id(1)
    i = pl.program_id(2)
    S = k_ref.shape[1]

    @pl.when(i == 0)
    def _():
        g = g_ref[0]
        lane = lax.broadcasted_iota(jnp.int32, g.shape, 1)
        for hh in range(2):
            c = jnp.sum(jnp.where(lane == 8 * p + hh, g, 0.0), axis=1, keepdims=True)
            nb_ref[hh] = jnp.broadcast_to(-c, (S, LANES))

    qT = qT_ref[0]
    row = lax.broadcasted_iota(jnp.int32, qT.shape, 0)
    zero = jnp.zeros_like(qT)
    qz = (jnp.where(row < HEAD_DIM, qT, zero), jnp.where(row >= HEAD_DIM, qT, zero))
    krow = lax.broadcasted_iota(jnp.int32, (tq, tq), 0)
    qcol = lax.broadcasted_iota(jnp.int32, (tq, tq), 1)
    causal = krow <= qcol

    def tile(j, carry, masked):
        r0 = pl.multiple_of(j * tq, tq)
        k = k_ref[0, pl.ds(r0, tq), :]
        vT = vT_ref[0, j]
        new = []
        for hh in range(2):
            m, l, acc = carry[hh]
            sT = jnp.dot(k, qz[hh], preferred_element_type=F32)
            nb = nb_ref[hh, pl.ds(r0, tq), :]
            z = sT + jnp.concatenate([nb] * (tq // LANES), axis=1)
            if masked:
                z = jnp.where(causal, z, NEG)
            m_new = jnp.maximum(m, jnp.max(z, axis=0, keepdims=True))
            alpha = jnp.exp(m - m_new)
            pT = jnp.exp(z - m_new)
            l_new = alpha * l + jnp.sum(pT, axis=0, keepdims=True)
            pv = jnp.dot(vT[hh * HEAD_DIM:(hh + 1) * HEAD_DIM, :], pT.astype(BF16), preferred_element_type=F32)
            new.append((m_new, l_new, alpha * acc + pv))
        return tuple(new)

    init = tuple((jnp.full((1, tq), NEG, F32), jnp.zeros((1, tq), F32), jnp.zeros((HEAD_DIM, tq), F32))
                 for _ in range(2))
    carry = lax.fori_loop(0, i, lambda j, c: tile(j, c, False), init)
    carry = tile(i, carry, True)
    oT = jnp.concatenate([carry[0][2] / carry[0][1], carry[1][2] / carry[1][1]], axis=0)
    o_ref[0] = oT.T.astype(BF16)


def _fox_attention(fqT, fk, fvT, gtok, *, tq):
    B, _, S = fqT.shape
    nkv = S // tq
    return pl.pallas_call(
        functools.partial(_fox_kernel, tq=tq),
        grid=(B, FOX_HEADS // 2, S // tq),
        in_specs=[
            pl.BlockSpec((1, LANES, tq), lambda b, p, i: (b, p, i)),
            pl.BlockSpec((1, S, LANES), lambda b, p, i: (b, 0, p)),
            pl.BlockSpec((1, nkv, LANES, tq), lambda b, p, i: (b, 0, p, 0)),
            pl.BlockSpec((1, S, LANES), lambda b, p, i: (b, 0, 0)),
        ],
        out_specs=pl.BlockSpec((1, tq, LANES), lambda b, p, i: (b, i, p)),
        out_shape=jax.ShapeDtypeStruct((B, S, MIX_WIDTH), BF16),
        scratch_shapes=[pltpu.VMEM((2, S, LANES), F32)],
        compiler_params=pltpu.CompilerParams(dimension_semantics=("parallel", "parallel", "arbitrary"),
                                             vmem_limit_bytes=VMEM_LIMIT_BYTES),
        name="fox_attention",
    )(fqT, fk, fvT, gtok)


def _swa_kernel(sinks_ref, q_ref, kp_ref, kc_ref, vp_ref, vc_ref, o_ref):
    n = pl.program_id(1)
    L = BLOCK
    row = lax.broadcasted_iota(jnp.int32, (L, 2 * L), 0)
    col = lax.broadcasted_iota(jnp.int32, (L, 2 * L), 1)
    row_prev = row + jnp.where(n == 0, 2 * L, 0)
    mask = ((col < L) & (col > row_prev)) | ((col >= L) & ((col - L) <= row))
    lane = lax.broadcasted_iota(jnp.int32, (L, LANES), 1)
    low = lane < HEAD_DIM
    q = q_ref[0]
    zero = jnp.zeros((L, LANES), BF16)
    for pair in range(SWA_Q_HEADS // 2):
        g = pair // 2
        kf = jnp.concatenate([kp_ref[0, :, g * LANES:(g + 1) * LANES], kc_ref[0, :, g * LANES:(g + 1) * LANES]], axis=0)
        vf = jnp.concatenate([vp_ref[0, :, g * LANES:(g + 1) * LANES], vc_ref[0, :, g * LANES:(g + 1) * LANES]], axis=0)
        qp = q[:, pair * LANES:(pair + 1) * LANES]
        outs = []
        for half in range(2):
            sink = sinks_ref[2 * pair + half]
            qh = jnp.where(low if half == 0 else ~low, qp, zero)
            z = jnp.where(mask, _nt_dot(qh, kf), NEG)
            m = jnp.maximum(jnp.max(z, axis=1, keepdims=True), sink)
            pm = jnp.exp(z - m)
            den = jnp.sum(pm, axis=1, keepdims=True) + jnp.exp(sink - m)
            outs.append(jnp.dot(pm.astype(BF16), vf, preferred_element_type=F32) / den)
        o_ref[0, :, pair * LANES:(pair + 1) * LANES] = jnp.where(low, outs[0], outs[1]).astype(BF16)


def _swa_attention(sinks, sq, sk, sv):
    B, S, _ = sq.shape
    prev = lambda b, n: (b, jnp.maximum(n - 1, 0), 0)
    cur = lambda b, n: (b, n, 0)
    return pl.pallas_call(
        _swa_kernel,
        grid=(B, S // BLOCK),
        in_specs=[
            pl.BlockSpec(memory_space=pltpu.SMEM),
            pl.BlockSpec((1, BLOCK, 512), cur),
            pl.BlockSpec((1, BLOCK, 256), prev), pl.BlockSpec((1, BLOCK, 256), cur),
            pl.BlockSpec((1, BLOCK, 256), prev), pl.BlockSpec((1, BLOCK, 256), cur),
        ],
        out_specs=pl.BlockSpec((1, BLOCK, 512), cur),
        out_shape=jax.ShapeDtypeStruct((B, S, MIX_WIDTH), BF16),
        compiler_params=pltpu.CompilerParams(dimension_semantics=("parallel", "parallel"),
                                             vmem_limit_bytes=VMEM_LIMIT_BYTES),
        name="swa_attention",
    )(sinks, sq, sk, sk, sv, sv)


def _mlstm_kernel(q_ref, k_ref, v_ref, og_ref, gt_ref, gT_ref, cw_ref, cb_ref, on_ref, y_ref,
                  c_ref, n_ref, m_ref, qbuf_ref, kbuf_ref):
    c = pl.program_id(1)
    L = BLOCK
    HALO = 8

    @pl.when(c == 0)
    def _():
        c_ref[...] = jnp.zeros_like(c_ref)
        n_ref[...] = jnp.zeros_like(n_ref)
        m_ref[...] = jnp.zeros_like(m_ref)
        qbuf_ref[0:HALO, :] = jnp.zeros((HALO, MIX_WIDTH), F32)
        kbuf_ref[0:HALO, :] = jnp.zeros((HALO, MIX_WIDTH), F32)

    def conv_silu(buf_ref, x_ref, off):
        buf_ref[HALO:HALO + L, :] = x_ref[0].astype(F32)
        y = cb_ref[:, off:off + MIX_WIDTH]
        for j in range(CONV_WIDTH):
            lo = HALO - (CONV_WIDTH - 1) + j
            y = y + buf_ref[lo:lo + L, :] * cw_ref[j:j + 1, off:off + MIX_WIDTH]
        buf_ref[0:HALO, :] = buf_ref[L:L + HALO, :]
        return y * _sigmoid(y)

    qc = conv_silu(qbuf_ref, q_ref, 0)
    kc = conv_silu(kbuf_ref, k_ref, MIX_WIDTH) * (MLSTM_HEAD_DIM ** -0.5)
    gt = gt_ref[0]
    gT = gT_ref[0, 0]
    tri = lax.broadcasted_iota(jnp.int32, (L, L), 0) >= lax.broadcasted_iota(jnp.int32, (L, L), 1)

    for hd in range(MLSTM_HEADS):
        sl = slice(hd * MLSTM_HEAD_DIM, (hd + 1) * MLSTM_HEAD_DIM)
        q = qc[:, sl]
        k = kc[:, sl]
        qb, kb = q.astype(BF16), k.astype(BF16)
        v = v_ref[0, :, sl]
        i_col, b_col = gt[:, _GI + hd:_GI + hd + 1], gt[:, _GF + hd:_GF + hd + 1]
        i_row, b_row = gT[_GI + hd:_GI + hd + 1, :], gT[_GF + hd:_GF + hd + 1, :]
        m_prev = m_ref[hd]
        cmat = c_ref[hd]
        n_row = n_ref[hd]
        b_last = b_col[L - 1:L, :]

        a_col = b_col + m_prev
        dmat = jnp.where(tri, b_col - b_row + i_row, NEG)
        mt = jnp.maximum(a_col, jnp.max(dmat, axis=1, keepdims=True))
        w_inter = jnp.exp(a_col - mt)
        smat = jnp.exp(dmat - mt) * _nt_dot(qb, kb)
        num = (w_inter * jnp.dot(qb, cmat.astype(BF16), preferred_element_type=F32)
               + jnp.dot(smat.astype(BF16), v, preferred_element_type=F32))
        den = w_inter * jnp.sum(q * n_row, axis=1, keepdims=True) + jnp.sum(smat, axis=1, keepdims=True)
        hcur = num / jnp.maximum(jnp.abs(den), jnp.exp(-mt))

        g_col = b_last - b_col + i_col
        g_row = b_last - b_row + i_row
        m_new = jnp.maximum(b_last + m_prev, jnp.max(g_row, axis=1, keepdims=True))
        decay = jnp.exp(b_last + m_prev - m_new)
        kw = jnp.exp(g_col - m_new) * k
        c_ref[hd] = decay * cmat + _tn_dot(kw.astype(BF16), v)
        n_ref[hd] = decay * n_row + jnp.sum(kw, axis=0, keepdims=True)
        m_ref[hd] = m_new

        hn = _rmsnorm_rows(hcur, on_ref[:, sl])
        y_ref[0, :, sl] = (hn * og_ref[0, :, sl].astype(F32)).astype(BF16)


def _mlstm(mq, mk, mv, og, gtok, gT, conv_w, conv_b, out_norm):
    B, S, _ = mq.shape
    tok = lambda w: pl.BlockSpec((1, BLOCK, w), lambda b, c: (b, c, 0))
    const = lambda shape: pl.BlockSpec(shape, lambda b, c: (0,) * len(shape))
    return pl.pallas_call(
        _mlstm_kernel,
        grid=(B, S // BLOCK),
        in_specs=[tok(512), tok(512), tok(512), tok(512), tok(LANES),
                  pl.BlockSpec((1, 1, LANES, BLOCK), lambda b, c: (b, c, 0, 0)),
                  const((CONV_WIDTH, 2 * MIX_WIDTH)), const((1, 2 * MIX_WIDTH)), const((1, MIX_WIDTH))],
        out_specs=tok(512),
        out_shape=jax.ShapeDtypeStruct((B, S, MIX_WIDTH), BF16),
        scratch_shapes=[
            pltpu.VMEM((MLSTM_HEADS, MLSTM_HEAD_DIM, MLSTM_HEAD_DIM), F32),
            pltpu.VMEM((MLSTM_HEADS, 1, MLSTM_HEAD_DIM), F32),
            pltpu.VMEM((MLSTM_HEADS, 1, 1), F32),
            pltpu.VMEM((BLOCK + 8, MIX_WIDTH), F32),
            pltpu.VMEM((BLOCK + 8, MIX_WIDTH), F32),
        ],
        compiler_params=pltpu.CompilerParams(dimension_semantics=("parallel", "arbitrary"),
                                             vmem_limit_bytes=VMEM_LIMIT_BYTES),
        name="mlstm",
    )(mq, mk, mv, og, gtok, gT, conv_w, conv_b, out_norm)


def _merge_kernel(x_ref, g_ref, yf_ref, ys_ref, ym_ref, wgl_ref, wb_ref, wo_ref, o_ref):
    x = x_ref[...]
    h = _rmsnorm_rows(x, g_ref[...]).astype(BF16)
    merged = None
    for br, y_ref in enumerate((yf_ref, ys_ref, ym_ref)):
        gate = _sigmoid(jnp.dot(h, wgl_ref[:, br * D_MODEL:(br + 1) * D_MODEL], preferred_element_type=F32))
        term = gate * jnp.dot(y_ref[...], wb_ref[br], preferred_element_type=F32)
        merged = term if merged is None else merged + term
    o_ref[...] = x + jnp.dot(merged.astype(BF16), wo_ref[...], preferred_element_type=F32)


def _merge(x2, gain, yf, ys, ym, wgl, wb, wo, *, tm):
    T, D = x2.shape
    row = lambda w: pl.BlockSpec((tm, w), lambda t: (t, 0))
    const = lambda shape: pl.BlockSpec(shape, lambda t: (0,) * len(shape))
    return pl.pallas_call(
        _merge_kernel,
        grid=(T // tm,),
        in_specs=[row(D), const((1, D)), row(512), row(512), row(512),
                  const(wgl.shape), const(wb.shape), const(wo.shape)],
        out_specs=row(D),
        out_shape=jax.ShapeDtypeStruct((T, D), F32),
        compiler_params=pltpu.CompilerParams(dimension_semantics=("parallel",),
                                             vmem_limit_bytes=VMEM_LIMIT_BYTES),
        name="merge_out",
    )(x2, gain, yf, ys, ym, wgl, wb, wo)


def _mlp_kernel(x_ref, g_ref, wu_ref, wd_ref, o_ref, *, fc):
    x = x_ref[...]
    h = _rmsnorm_rows(x, g_ref[...]).astype(BF16)
    acc = x
    for f in range(D_FF // fc):
        a = jnp.maximum(jnp.dot(h, wu_ref[:, f * fc:(f + 1) * fc], preferred_element_type=F32), 0.0)
        acc = acc + jnp.dot((a * a).astype(BF16), wd_ref[f * fc:(f + 1) * fc, :], preferred_element_type=F32)
    o_ref[...] = acc


def _mlp(x2, gain, wu, wd, *, tm, fc):
    T, D = x2.shape
    row = pl.BlockSpec((tm, D), lambda t: (t, 0))
    const = lambda shape: pl.BlockSpec(shape, lambda t: (0,) * len(shape))
    return pl.pallas_call(
        functools.partial(_mlp_kernel, fc=fc),
        grid=(T // tm,),
        in_specs=[row, const((1, D)), const(wu.shape), const(wd.shape)],
        out_specs=row,
        out_shape=jax.ShapeDtypeStruct((T, D), F32),
        compiler_params=pltpu.CompilerParams(dimension_semantics=("parallel",),
                                             vmem_limit_bytes=VMEM_LIMIT_BYTES),
        name="mlp",
    )(x2, gain, wu, wd)


def _tiles(S):
    tm = 512 if S % 512 == 0 else S
    tq = 256 if S % 256 == 0 else S
    return tm, tq


def _layer_params(w_in, fox_f_bias, fox_q_norm, fox_k_norm, swa_q_norm, swa_k_norm, mlstm_i_bias, mlstm_f_bias):
    offs = np.cumsum(IN_SPLITS)[:-1].tolist()
    fq, fk, fv, ff, sq, sk, sv, mq, mk, mv, mi, mf, mo, gl = jnp.split(w_in, offs, axis=1)
    dup = lambda w: jnp.concatenate([w[:, :HEAD_DIM], w[:, :HEAD_DIM], w[:, HEAD_DIM:], w[:, HEAD_DIM:]], axis=1)
    gates = jnp.zeros((D_MODEL, LANES), F32)
    gbias = jnp.zeros((1, LANES), F32)
    for hh in range(FOX_HEADS):
        colx = 8 * (hh // 2) + hh % 2
        gates = gates.at[:, colx].set(ff[:, hh])
        gbias = gbias.at[0, colx].set(fox_f_bias[hh])
    gates = gates.at[:, _GI:_GI + MLSTM_HEADS].set(mi).at[:, _GF:_GF + MLSTM_HEADS].set(mf)
    gbias = gbias.at[0, _GI:_GI + MLSTM_HEADS].set(mlstm_i_bias).at[0, _GF:_GF + MLSTM_HEADS].set(mlstm_f_bias)
    wt = jnp.concatenate([fk, sq, dup(sk), dup(sv), mq, mk, mv, mo, gates], axis=1).astype(BF16)
    wT = jnp.concatenate([fq, fv], axis=1).T.astype(BF16)
    fkg = jnp.tile(fox_k_norm * fox_q_norm * (HEAD_DIM ** -0.5), FOX_HEADS)[None, :]
    sqg = jnp.tile(swa_q_norm * (HEAD_DIM ** -0.5), SWA_Q_HEADS)[None, :]
    skg = jnp.tile(swa_k_norm, 2 * SWA_KV_HEADS)[None, :]
    return wt, wT, fkg, sqg, skg, gbias, gl.astype(BF16)


def kernel(x, norm_mix, w_in, fox_f_bias, fox_q_norm, fox_k_norm, swa_q_norm, swa_k_norm, swa_sinks, conv_w,
           conv_b, mlstm_i_bias, mlstm_f_bias, mlstm_out_norm, w_branch, w_out, norm_mlp, w_up, w_down):
    B, S, D = x.shape
    depth = w_in.shape[0]
    tm, tq = _tiles(S)

    inv = ROPE_THETA ** (-jnp.arange(0, HEAD_DIM, 2, dtype=F32) / HEAD_DIM)
    ang = jnp.arange(S, dtype=F32)[:, None] * inv[None, :]
    cos, sin = jnp.cos(ang), jnp.sin(ang)
    cosf = jnp.tile(jnp.concatenate([cos, cos], axis=1), (1, LANES // HEAD_DIM))
    sinf = jnp.tile(jnp.concatenate([-sin, sin], axis=1), (1, LANES // HEAD_DIM))
    gm = jnp.asarray(np.kron(np.eye(MIX_WIDTH // HEAD_DIM), np.full((HEAD_DIM, HEAD_DIM), 1.0 / HEAD_DIM)), BF16)
    ltri = jnp.asarray(np.tril(np.ones((BLOCK, BLOCK))), BF16)

    for l in range(depth):
        wt, wT, fkg, sqg, skg, gbias, wgl = _layer_params(
            w_in[l], fox_f_bias[l], fox_q_norm[l], fox_k_norm[l], swa_q_norm[l], swa_k_norm[l],
            mlstm_i_bias[l], mlstm_f_bias[l])
        (fqT, fvT, fk, sq, sk, sv, mq, mk, mv, og, gtok, gT) = _in_proj(
            x, norm_mix[l][None, :], wt, wT, fkg, sqg, skg, gm, cosf, sinf, gbias, ltri, tm=tm, tkv=tq)
        y_fox = _fox_attention(fqT, fk, fvT, gtok, tq=tq)
        y_swa = _swa_attention(swa_sinks[l], sq, sk, sv)
        y_ml = _mlstm(mq, mk, mv, og, gtok, gT, conv_w[l], conv_b[l][None, :], mlstm_out_norm[l][None, :])
        x2 = _merge(x.reshape(B * S, D), norm_mix[l][None, :], y_fox.reshape(B * S, -1), y_swa.reshape(B * S, -1),
                    y_ml.reshape(B * S, -1), wgl, w_branch[l].astype(BF16), w_out[l].astype(BF16), tm=tm)
        x2 = _mlp(x2, norm_mlp[l][None, :], w_up[l].astype(BF16), w_down[l].astype(BF16), tm=tm, fc=1024)
        x = x2.reshape(B, S, D)
    return x
```

```python
import functools

import jax
import jax.numpy as jnp
import numpy as np
from jax import lax
from jax.experimental import pallas as pl
from jax.experimental.pallas import tpu as pltpu

F32 = jnp.float32
BF16 = jnp.bfloat16

D_MODEL = 1024
HEAD_DIM = 64
MIX_WIDTH = 512
FOX_HEADS = 8
SWA_Q_HEADS = 8
SWA_KV_HEADS = 2
SWA_WINDOW = 128
MLSTM_HEADS = 4
MLSTM_HEAD_DIM = 128
CONV_WIDTH = 4
D_FF = 4 * D_MODEL
N_BRANCHES = 3
BLOCK = 128
ROPE_THETA = 10000.0
EPS = 1e-6
NEG = -1e30
LOG2E = 1.4426950408889634
FOX_GROUP = 2

LANES = 128
FOX_VROWS = LANES + 16
VMEM_LIMIT_BYTES = 56 * 1024 * 1024

_FK, _SQ, _SK, _SV, _MQ, _MK, _MV, _MO, _GT, _NTOK = 0, 512, 1024, 1280, 1536, 2048, 2560, 3072, 3584, 3712
_GI, _GF = 32, 36

IN_SPLITS = (512, 512, 512, 8, 512, 128, 128, 512, 512, 512, 4, 4, 512, 3 * D_MODEL)


def _nt_dot(a, b):
    return lax.dot_general(a, b, (((1,), (1,)), ((), ())), preferred_element_type=F32)


def _tn_dot(a, b):
    return lax.dot_general(a, b, (((0,), (0,)), ((), ())), preferred_element_type=F32)


def _sigmoid(x):
    return 1.0 / (1.0 + jnp.exp(-x))


def _log_sigmoid(x):
    return jnp.minimum(x, 0.0) - jnp.log(1.0 + jnp.exp(-jnp.abs(x)))


def _rmsnorm_rows(x, gain_row):
    ms = jnp.mean(x * x, axis=-1, keepdims=True)
    return x * lax.rsqrt(ms + EPS) * gain_row


def _in_proj_kernel(x_ref, g_ref, wt_ref, wT_ref, fkg_ref, sqg_ref, skg_ref, gm_ref, cos_ref, sin_ref,
                    gbias_ref, ltri_ref,
                    fqT_ref, fvT_ref, fk_ref, sq_ref, sk_ref, sv_ref, mq_ref, mk_ref, mv_ref, og_ref,
                    gtok_ref, gT_ref, carry_ref, *, tm, tkv):
    s = pl.program_id(1)
    h = _rmsnorm_rows(x_ref[0], g_ref[...]).astype(BF16)

    def proj(lo, hi):
        return jnp.dot(h, wt_ref[:, lo:hi], preferred_element_type=F32)

    def head_ms(u):
        w = u.shape[1]
        return jnp.dot((u * u).astype(BF16), gm_ref[:w, :w], preferred_element_type=F32)

    def rope(u):
        lane = lax.broadcasted_iota(jnp.int32, (tm, LANES), 1)
        first = (lane % HEAD_DIM) < (HEAD_DIM // 2)
        c, sn = cos_ref[...], sin_ref[...]
        outs = []
        for j in range(u.shape[1] // LANES):
            uj = u[:, j * LANES:(j + 1) * LANES]
            partner = jnp.where(first, pltpu.roll(uj, LANES - HEAD_DIM // 2, 1), pltpu.roll(uj, HEAD_DIM // 2, 1))
            outs.append(uj * c + partner * sn)
        return jnp.concatenate(outs, axis=1)

    qT = _nt_dot(wT_ref[0:MIX_WIDTH, :], h)
    msT = jnp.dot(gm_ref[...], (qT * qT).astype(BF16), preferred_element_type=F32)
    qn = (qT * lax.rsqrt(msT + EPS)).astype(BF16)
    pad = jnp.zeros((HEAD_DIM, tm), BF16)
    for hd in range(FOX_HEADS):
        rows = qn[hd * HEAD_DIM:(hd + 1) * HEAD_DIM]
        fqT_ref[0, hd] = jnp.concatenate([rows, pad] if hd % 2 == 0 else [pad, rows], axis=0)
    vT = _nt_dot(wT_ref[MIX_WIDTH:2 * MIX_WIDTH, :], h).astype(BF16)
    ones = jnp.ones((FOX_VROWS - LANES, tkv), BF16)
    for c in range(tm // tkv):
        for pr in range(FOX_HEADS // 2):
            fvT_ref[0, c, pr] = jnp.concatenate([vT[pr * LANES:(pr + 1) * LANES, c * tkv:(c + 1) * tkv], ones], axis=0)

    u = proj(_FK, _SQ)
    kn = (u * lax.rsqrt(head_ms(u) + EPS) * fkg_ref[...]).astype(BF16)
    for pr in range(FOX_HEADS // 2):
        fk_ref[0, pr] = kn[:, pr * LANES:(pr + 1) * LANES]

    u = proj(_SQ, _SK)
    sq_ref[0] = rope(u * lax.rsqrt(head_ms(u) + EPS) * sqg_ref[...]).astype(BF16)
    u = proj(_SK, _SV)
    sk_ref[0] = rope(u * lax.rsqrt(head_ms(u) + EPS) * skg_ref[...]).astype(BF16)
    sv_ref[0] = proj(_SV, _MQ).astype(BF16)

    mq_ref[0] = proj(_MQ, _MK).astype(BF16)
    mk_ref[0] = proj(_MK, _MV).astype(BF16)
    mv_ref[0] = proj(_MV, _MO).astype(BF16)
    og_ref[0] = _sigmoid(proj(_MO, _GT)).astype(BF16)

    @pl.when(s == 0)
    def _():
        carry_ref[...] = jnp.zeros_like(carry_ref)

    g = proj(_GT, _NTOK) + gbias_ref[...]
    col = lax.broadcasted_iota(jnp.int32, (BLOCK, LANES), 1)
    is_i = (col >= _GI) & (col < _GF)
    is_fox = col < _GI
    ltri = ltri_ref[...]
    for c in range(tm // BLOCK):
        gc = g[c * BLOCK:(c + 1) * BLOCK]
        val = _log_sigmoid(gc)
        hi = val.astype(BF16)
        r1 = val - hi.astype(F32)
        mid = r1.astype(BF16)
        lo = (r1 - mid.astype(F32)).astype(BF16)
        cs = (jnp.dot(ltri, hi, preferred_element_type=F32) + jnp.dot(ltri, mid, preferred_element_type=F32)
              + jnp.dot(ltri, lo, preferred_element_type=F32))
        cs = cs + jnp.where(is_fox, carry_ref[...], 0.0)
        out_c = jnp.where(is_i, gc, cs)
        carry_ref[...] = out_c[BLOCK - 1:BLOCK, :]
        gtok_ref[0, c * BLOCK:(c + 1) * BLOCK, :] = out_c
        gT_ref[0, c] = out_c.T


def _in_proj(x, gain, wt, wT, fkg, sqg, skg, gm, cosf, sinf, gbias, ltri, *, tm, tkv):
    B, S, D = x.shape
    nc = S // BLOCK
    const = lambda shape: pl.BlockSpec(shape, lambda b, s: (0,) * len(shape))
    tok = lambda w: pl.BlockSpec((1, tm, w), lambda b, s: (b, s, 0))
    out_shape = (
        jax.ShapeDtypeStruct((B, FOX_HEADS, LANES, S), BF16),
        jax.ShapeDtypeStruct((B, S // tkv, FOX_HEADS // 2, FOX_VROWS, tkv), BF16),
        jax.ShapeDtypeStruct((B, FOX_HEADS // 2, S, LANES), BF16),
        jax.ShapeDtypeStruct((B, S, 512), BF16),
        jax.ShapeDtypeStruct((B, S, 256), BF16),
        jax.ShapeDtypeStruct((B, S, 256), BF16),
        jax.ShapeDtypeStruct((B, S, 512), BF16),
        jax.ShapeDtypeStruct((B, S, 512), BF16),
        jax.ShapeDtypeStruct((B, S, 512), BF16),
        jax.ShapeDtypeStruct((B, S, 512), BF16),
        jax.ShapeDtypeStruct((B, S, LANES), F32),
        jax.ShapeDtypeStruct((B, nc, LANES, BLOCK), F32),
    )
    out_specs = (
        pl.BlockSpec((1, FOX_HEADS, LANES, tm), lambda b, s: (b, 0, 0, s)),
        pl.BlockSpec((1, tm // tkv, FOX_HEADS // 2, FOX_VROWS, tkv), lambda b, s: (b, s, 0, 0, 0)),
        pl.BlockSpec((1, FOX_HEADS // 2, tm, LANES), lambda b, s: (b, 0, s, 0)),
        tok(512), tok(256), tok(256), tok(512), tok(512), tok(512), tok(512), tok(LANES),
        pl.BlockSpec((1, tm // BLOCK, LANES, BLOCK), lambda b, s: (b, s, 0, 0)),
    )
    in_specs = [
        pl.BlockSpec((1, tm, D), lambda b, s: (b, s, 0)),
        const((1, D)), const(wt.shape), const(wT.shape), const((1, 512)), const((1, 512)), const((1, 256)),
        const((512, 512)),
        pl.BlockSpec((tm, LANES), lambda b, s: (s, 0)), pl.BlockSpec((tm, LANES), lambda b, s: (s, 0)),
        const((1, LANES)), const((BLOCK, BLOCK)),
    ]
    return pl.pallas_call(
        functools.partial(_in_proj_kernel, tm=tm, tkv=tkv),
        grid=(B, S // tm), in_specs=in_specs, out_specs=out_specs, out_shape=out_shape,
        scratch_shapes=[pltpu.VMEM((1, LANES), F32)],
        compiler_params=pltpu.CompilerParams(dimension_semantics=("parallel", "arbitrary"),
                                             vmem_limit_bytes=VMEM_LIMIT_BYTES),
        name="in_proj",
    )(x, gain, wt, wT, fkg, sqg, skg, gm, cosf, sinf, gbias, ltri)


def _col_reduce(x, op, ways=8):
    rows, cols = x.shape
    return op(op(x.reshape(ways, rows // ways, cols), axis=0), axis=0, keepdims=True)


def _fox_kernel(qT_ref, k_ref, vT_ref, g_ref, o_ref, aug_ref, qq_ref, m_ref, l_ref, acc_ref, za_ref, zb_ref, p_ref,
                mt_ref, al1_ref, al2_ref, *, tq):
    i = pl.program_id(1)
    S = k_ref.shape[2]
    NP = FOX_HEADS // 2
    G = FOX_GROUP
    NAUG = 3

    @pl.when(i == 0)
    def _():
        lane = lax.broadcasted_iota(jnp.int32, (S, LANES), 1)
        for pr in range(NP):
            aug = jnp.zeros((S, LANES), F32)
            for hh in range(2):
                rest = g_ref[0, :, 8 * pr + hh:8 * pr + hh + 1] * (-LOG2E)
                for piece in range(NAUG):
                    part = rest.astype(BF16).astype(F32)
                    aug = jnp.where(lane == NAUG * hh + piece, part, aug)
                    rest = rest - part
            aug_ref[pr] = aug.astype(BF16)

    row = lax.broadcasted_iota(jnp.int32, (LANES, 2 * tq), 0)
    col = lax.broadcasted_iota(jnp.int32, (LANES, 2 * tq), 1)
    ones_rows = jnp.where((row < 2 * NAUG) & ((row >= NAUG) == (col >= tq)), 1.0, 0.0).astype(BF16)
    for pr in range(NP):
        qq_ref[pr] = jnp.concatenate(
            [jnp.concatenate([qT_ref[0, 2 * pr], qT_ref[0, 2 * pr + 1]], axis=1), ones_rows], axis=0)
    m_ref[...] = jnp.full(m_ref.shape, NEG, F32)
    l_ref[...] = jnp.zeros(l_ref.shape, F32)
    acc_ref[...] = jnp.zeros(acc_ref.shape, F32)
    za_ref[...] = jnp.full(za_ref.shape, NEG, F32)
    zb_ref[...] = jnp.full(zb_ref.shape, NEG, F32)
    p_ref[...] = jnp.zeros(p_ref.shape, BF16)
    mt_ref[...] = jnp.full(mt_ref.shape, NEG, F32)
    al1_ref[...] = jnp.ones(al1_ref.shape, F32)
    al2_ref[...] = jnp.ones(al2_ref.shape, F32)
    krow = lax.broadcasted_iota(jnp.int32, (tq, 2 * tq), 0)
    qcol = lax.broadcasted_iota(jnp.int32, (tq, 2 * tq), 1)
    causal = krow <= (qcol & (tq - 1))
    n_end = NP * (i + 1)

    def where_is(n):
        tile = jnp.clip(n >> 2, 0, i)
        return n & (NP - 1), tile, pl.multiple_of(tile * tq, tq)

    def stage_scores(n, j, guard):
        pr, _, r0 = where_is(n)
        kk = jnp.concatenate([k_ref[0, pr, pl.ds(r0, tq), :], aug_ref[pr, pl.ds(r0, tq), :]], axis=1)
        s = jnp.dot(kk, qq_ref[pr], preferred_element_type=F32)
        za_ref[j] = jnp.where(n < n_end, s, NEG) if guard else s

    def stage_max(n, j, masked):
        pr, _, _ = where_is(n)
        z = za_ref[j]
        if masked:
            z = jnp.where(causal, z, NEG)
        zb_ref[j] = z
        m = m_ref[pr]
        m_new = jnp.maximum(m, _col_reduce(z, jnp.max))
        al1_ref[j] = jnp.exp2(m - m_new)
        mt_ref[j] = m_new
        m_ref[pr] = m_new

    def stage_probs(n, j):
        p_ref[j] = jnp.exp2(zb_ref[j] - mt_ref[j]).astype(BF16)
        al2_ref[j] = al1_ref[j]

    def stage_values(n, j):
        pr, tile, _ = where_is(n)
        pv = jnp.dot(vT_ref[0, tile, pr], p_ref[j], preferred_element_type=F32)
        al = al2_ref[j]
        acc_ref[2 * pr] = al[:, :tq] * acc_ref[2 * pr] + pv[:HEAD_DIM, :tq]
        acc_ref[2 * pr + 1] = al[:, tq:] * acc_ref[2 * pr + 1] + pv[HEAD_DIM:2 * HEAD_DIM, tq:]
        l_ref[pr] = al * l_ref[pr] + pv[2 * HEAD_DIM:2 * HEAD_DIM + 1, :]

    def step(g, masked, guard):
        for j in range(G):
            stage_values((g - 3) * G + j, j)
        for j in range(G):
            stage_probs((g - 2) * G + j, j)
        for j in range(G):
            stage_max((g - 1) * G + j, j, masked)
        for j in range(G):
            stage_scores(g * G + j, j, guard)

    def body_plain(g, carry):
        step(g, False, False)
        return carry

    def body_diag(g, carry):
        step(g, True, True)
        return carry

    g_diag = (NP // G) * i + 1
    lax.fori_loop(0, g_diag, body_plain, 0)
    lax.fori_loop(g_diag, (NP // G) * (i + 1) + 3, body_diag, 0)
    for pr in range(NP):
        la, lb = l_ref[pr][:, :tq], l_ref[pr][:, tq:]
        oT = jnp.concatenate([acc_ref[2 * pr] / la, acc_ref[2 * pr + 1] / lb], axis=0)
        o_ref[0, :, pr * LANES:(pr + 1) * LANES] = oT.T.astype(BF16)


def _fox_attention(fqT, fk, fvT, gtok, *, tq):
    B, NP, S, _ = fk.shape
    nkv = S // tq
    vrows = fvT.shape[3]
    return pl.pallas_call(
        functools.partial(_fox_kernel, tq=tq),
        grid=(B, S // tq),
        in_specs=[
            pl.BlockSpec((1, FOX_HEADS, LANES, tq), lambda b, i: (b, 0, 0, i)),
            pl.BlockSpec((1, NP, S, LANES), lambda b, i: (b, 0, 0, 0)),
            pl.BlockSpec((1, nkv, NP, vrows, tq), lambda b, i: (b, 0, 0, 0, 0)),
            pl.BlockSpec((1, S, LANES), lambda b, i: (b, 0, 0)),
        ],
        out_specs=pl.BlockSpec((1, tq, MIX_WIDTH), lambda b, i: (b, i, 0)),
        out_shape=jax.ShapeDtypeStruct((B, S, MIX_WIDTH), BF16),
        scratch_shapes=[
            pltpu.VMEM((NP, S, LANES), BF16),
            pltpu.VMEM((NP, 2 * LANES, 2 * tq), BF16),
            pltpu.VMEM((NP, 1, 2 * tq), F32),
            pltpu.VMEM((NP, 1, 2 * tq), F32),
            pltpu.VMEM((FOX_HEADS, HEAD_DIM, tq), F32),
            pltpu.VMEM((FOX_GROUP, tq, 2 * tq), F32),
            pltpu.VMEM((FOX_GROUP, tq, 2 * tq), F32),
            pltpu.VMEM((FOX_GROUP, tq, 2 * tq), BF16),
            pltpu.VMEM((FOX_GROUP, 1, 2 * tq), F32),
            pltpu.VMEM((FOX_GROUP, 1, 2 * tq), F32),
            pltpu.VMEM((FOX_GROUP, 1, 2 * tq), F32),
        ],
        compiler_params=pltpu.CompilerParams(dimension_semantics=("parallel", "arbitrary"),
                                             vmem_limit_bytes=VMEM_LIMIT_BYTES),
        name="fox_attention",
    )(fqT, fk, fvT, gtok)


def _swa_kernel(sinks_ref, q_ref, kp_ref, kc_ref, vp_ref, vc_ref, o_ref):
    n = pl.program_id(1)
    L = BLOCK
    row = lax.broadcasted_iota(jnp.int32, (L, 2 * L), 0)
    col = lax.broadcasted_iota(jnp.int32, (L, 2 * L), 1)
    row_prev = row + jnp.where(n == 0, 2 * L, 0)
    mask = ((col < L) & (col > row_prev)) | ((col >= L) & ((col - L) <= row))
    lane = lax.broadcasted_iota(jnp.int32, (L, LANES), 1)
    low = lane < HEAD_DIM
    q = q_ref[0]
    zero = jnp.zeros((L, LANES), BF16)
    for pair in range(SWA_Q_HEADS // 2):
        g = pair // 2
        kf = jnp.concatenate([kp_ref[0, :, g * LANES:(g + 1) * LANES], kc_ref[0, :, g * LANES:(g + 1) * LANES]], axis=0)
        vf = jnp.concatenate([vp_ref[0, :, g * LANES:(g + 1) * LANES], vc_ref[0, :, g * LANES:(g + 1) * LANES]], axis=0)
        qp = q[:, pair * LANES:(pair + 1) * LANES]
        outs = []
        for half in range(2):
            sink = sinks_ref[2 * pair + half]
            qh = jnp.where(low if half == 0 else ~low, qp, zero)
            z = jnp.where(mask, _nt_dot(qh, kf), NEG)
            m = jnp.maximum(jnp.max(z, axis=1, keepdims=True), sink)
            pm = jnp.exp(z - m)
            den = jnp.sum(pm, axis=1, keepdims=True) + jnp.exp(sink - m)
            outs.append(jnp.dot(pm.astype(BF16), vf, preferred_element_type=F32) / den)
        o_ref[0, :, pair * LANES:(pair + 1) * LANES] = jnp.where(low, outs[0], outs[1]).astype(BF16)


def _swa_attention(sinks, sq, sk, sv):
    B, S, _ = sq.shape
    prev = lambda b, n: (b, jnp.maximum(n - 1, 0), 0)
    cur = lambda b, n: (b, n, 0)
    return pl.pallas_call(
        _swa_kernel,
        grid=(B, S // BLOCK),
        in_specs=[
            pl.BlockSpec(memory_space=pltpu.SMEM),
            pl.BlockSpec((1, BLOCK, 512), cur),
            pl.BlockSpec((1, BLOCK, 256), prev), pl.BlockSpec((1, BLOCK, 256), cur),
            pl.BlockSpec((1, BLOCK, 256), prev), pl.BlockSpec((1, BLOCK, 256), cur),
        ],
        out_specs=pl.BlockSpec((1, BLOCK, 512), cur),
        out_shape=jax.ShapeDtypeStruct((B, S, MIX_WIDTH), BF16),
        compiler_params=pltpu.CompilerParams(dimension_semantics=("parallel", "parallel"),
                                             vmem_limit_bytes=VMEM_LIMIT_BYTES),
        name="swa_attention",
    )(sinks, sq, sk, sk, sv, sv)


def _mlstm_kernel(qp_ref, q_ref, kp_ref, k_ref, v_ref, og_ref, gt_ref, gT_ref, sh_ref, cw_ref, cb_ref, on_ref,
                  y_ref, c_ref, n_ref, m_ref):
    c = pl.program_id(1)
    L = BLOCK

    @pl.when(c == 0)
    def _():
        c_ref[...] = jnp.zeros_like(c_ref)
        n_ref[...] = jnp.zeros_like(n_ref)
        m_ref[...] = jnp.zeros_like(m_ref)

    def conv_silu(p_ref, x_ref, off):
        cur = x_ref[0]
        prev = jnp.where(c > 0, p_ref[0], jnp.zeros_like(cur))
        sh = jnp.dot(sh_ref[...], jnp.concatenate([prev, cur], axis=0), preferred_element_type=F32)
        y = cb_ref[:, off:off + MIX_WIDTH]
        for j in range(CONV_WIDTH - 1):
            y = y + sh[j * L:(j + 1) * L] * cw_ref[j:j + 1, off:off + MIX_WIDTH]
        y = y + cur.astype(F32) * cw_ref[CONV_WIDTH - 1:CONV_WIDTH, off:off + MIX_WIDTH]
        return y * _sigmoid(y)

    qc = conv_silu(qp_ref, q_ref, 0)
    kc = conv_silu(kp_ref, k_ref, MIX_WIDTH) * (MLSTM_HEAD_DIM ** -0.5)
    gt = gt_ref[0]
    gT = gT_ref[0, 0]
    tri = lax.broadcasted_iota(jnp.int32, (L, L), 0) >= lax.broadcasted_iota(jnp.int32, (L, L), 1)

    for hd in range(MLSTM_HEADS):
        sl = slice(hd * MLSTM_HEAD_DIM, (hd + 1) * MLSTM_HEAD_DIM)
        q = qc[:, sl]
        k = kc[:, sl]
        qb, kb = q.astype(BF16), k.astype(BF16)
        v = v_ref[0, :, sl]
        i_col, b_col = gt[:, _GI + hd:_GI + hd + 1], gt[:, _GF + hd:_GF + hd + 1]
        i_row, b_row = gT[_GI + hd:_GI + hd + 1, :], gT[_GF + hd:_GF + hd + 1, :]
        b_last = b_col[L - 1:L, :]

        dmat = jnp.where(tri, b_col - b_row + i_row, NEG)
        m_loc = jnp.max(dmat, axis=1, keepdims=True)
        smat = jnp.exp(dmat - m_loc) * _nt_dot(qb, kb)
        s_v = jnp.dot(smat.astype(BF16), v, preferred_element_type=F32)
        s_sum = jnp.sum(smat, axis=1, keepdims=True)
        g_col = b_last - b_col + i_col
        g_max = jnp.max(b_last - b_row + i_row, axis=1, keepdims=True)
        kw = jnp.exp(g_col - g_max) * k
        kv_new = _tn_dot(kw.astype(BF16), v)
        k_new = jnp.sum(kw, axis=0, keepdims=True)

        m_prev = m_ref[hd]
        cmat = c_ref[hd]
        n_row = n_ref[hd]
        a_col = b_col + m_prev
        mt = jnp.maximum(a_col, m_loc)
        w_inter = jnp.exp(a_col - mt)
        w_intra = jnp.exp(m_loc - mt)
        num = w_inter * jnp.dot(qb, cmat.astype(BF16), preferred_element_type=F32) + w_intra * s_v
        den = w_inter * jnp.sum(q * n_row, axis=1, keepdims=True) + w_intra * s_sum
        hcur = num / jnp.maximum(jnp.abs(den), jnp.exp(-mt))

        m_new = jnp.maximum(b_last + m_prev, g_max)
        decay = jnp.exp(b_last + m_prev - m_new)
        w_new = jnp.exp(g_max - m_new)
        c_ref[hd] = decay * cmat + w_new * kv_new
        n_ref[hd] = decay * n_row + w_new * k_new
        m_ref[hd] = m_new

        hn = _rmsnorm_rows(hcur, on_ref[:, sl])
        y_ref[0, :, sl] = (hn * og_ref[0, :, sl].astype(F32)).astype(BF16)


def _mlstm(mq, mk, mv, og, gtok, gT, shifts, conv_w, conv_b, out_norm):
    B, S, _ = mq.shape
    cur = lambda b, c: (b, c, 0)
    prev = lambda b, c: (b, jnp.maximum(c - 1, 0), 0)
    tok = lambda w: pl.BlockSpec((1, BLOCK, w), cur)
    const = lambda shape: pl.BlockSpec(shape, lambda b, c: (0,) * len(shape))
    return pl.pallas_call(
        _mlstm_kernel,
        grid=(B, S // BLOCK),
        in_specs=[pl.BlockSpec((1, BLOCK, 512), prev), tok(512), pl.BlockSpec((1, BLOCK, 512), prev), tok(512),
                  tok(512), tok(512), tok(LANES),
                  pl.BlockSpec((1, 1, LANES, BLOCK), lambda b, c: (b, c, 0, 0)),
                  const(shifts.shape), const((CONV_WIDTH, 2 * MIX_WIDTH)), const((1, 2 * MIX_WIDTH)),
                  const((1, MIX_WIDTH))],
        out_specs=tok(512),
        out_shape=jax.ShapeDtypeStruct((B, S, MIX_WIDTH), BF16),
        scratch_shapes=[
            pltpu.VMEM((MLSTM_HEADS, MLSTM_HEAD_DIM, MLSTM_HEAD_DIM), F32),
            pltpu.VMEM((MLSTM_HEADS, 1, MLSTM_HEAD_DIM), F32),
            pltpu.VMEM((MLSTM_HEADS, 1, 1), F32),
        ],
        compiler_params=pltpu.CompilerParams(dimension_semantics=("parallel", "arbitrary"),
                                             vmem_limit_bytes=VMEM_LIMIT_BYTES),
        name="mlstm",
    )(mq, mq, mk, mk, mv, og, gtok, gT, shifts, conv_w, conv_b, out_norm)


def _merge_kernel(x_ref, g_ref, yf_ref, ys_ref, ym_ref, wgl_ref, wb_ref, wo_ref, o_ref):
    x = x_ref[...]
    h = _rmsnorm_rows(x, g_ref[...]).astype(BF16)
    merged = None
    for br, y_ref in enumerate((yf_ref, ys_ref, ym_ref)):
        gate = _sigmoid(jnp.dot(h, wgl_ref[:, br * D_MODEL:(br + 1) * D_MODEL], preferred_element_type=F32))
        term = gate * jnp.dot(y_ref[...], wb_ref[br], preferred_element_type=F32)
        merged = term if merged is None else merged + term
    o_ref[...] = x + jnp.dot(merged.astype(BF16), wo_ref[...], preferred_element_type=F32)


def _merge(x2, gain, yf, ys, ym, wgl, wb, wo, *, tm):
    T, D = x2.shape
    row = lambda w: pl.BlockSpec((tm, w), lambda t: (t, 0))
    const = lambda shape: pl.BlockSpec(shape, lambda t: (0,) * len(shape))
    return pl.pallas_call(
        _merge_kernel,
        grid=(T // tm,),
        in_specs=[row(D), const((1, D)), row(512), row(512), row(512),
                  const(wgl.shape), const(wb.shape), const(wo.shape)],
        out_specs=row(D),
        out_shape=jax.ShapeDtypeStruct((T, D), F32),
        compiler_params=pltpu.CompilerParams(dimension_semantics=("parallel",),
                                             vmem_limit_bytes=VMEM_LIMIT_BYTES),
        name="merge_out",
    )(x2, gain, yf, ys, ym, wgl, wb, wo)


def _mlp_kernel(x_ref, g_ref, wu_ref, wd_ref, o_ref, *, fc):
    x = x_ref[...]
    h = _rmsnorm_rows(x, g_ref[...]).astype(BF16)
    acc = x
    for f in range(D_FF // fc):
        a = jnp.maximum(jnp.dot(h, wu_ref[:, f * fc:(f + 1) * fc], preferred_element_type=F32), 0.0)
        acc = acc + jnp.dot((a * a).astype(BF16), wd_ref[f * fc:(f + 1) * fc, :], preferred_element_type=F32)
    o_ref[...] = acc


def _mlp(x2, gain, wu, wd, *, tm, fc):
    T, D = x2.shape
    row = pl.BlockSpec((tm, D), lambda t: (t, 0))
    const = lambda shape: pl.BlockSpec(shape, lambda t: (0,) * len(shape))
    return pl.pallas_call(
        functools.partial(_mlp_kernel, fc=fc),
        grid=(T // tm,),
        in_specs=[row, const((1, D)), const(wu.shape), const(wd.shape)],
        out_specs=row,
        out_shape=jax.ShapeDtypeStruct((T, D), F32),
        compiler_params=pltpu.CompilerParams(dimension_semantics=("parallel",),
                                             vmem_limit_bytes=VMEM_LIMIT_BYTES),
        name="mlp",
    )(x2, gain, wu, wd)


def _tiles(S):
    tm = 512 if S % 512 == 0 else S
    tq = 256 if S % 256 == 0 else S
    return tm, tq


def _layer_params(w_in, fox_f_bias, fox_q_norm, fox_k_norm, swa_q_norm, swa_k_norm, mlstm_i_bias, mlstm_f_bias):
    offs = np.cumsum(IN_SPLITS)[:-1].tolist()
    fq, fk, fv, ff, sq, sk, sv, mq, mk, mv, mi, mf, mo, gl = jnp.split(w_in, offs, axis=1)
    dup = lambda w: jnp.concatenate([w[:, :HEAD_DIM], w[:, :HEAD_DIM], w[:, HEAD_DIM:], w[:, HEAD_DIM:]], axis=1)
    gates = jnp.zeros((D_MODEL, LANES), F32)
    gbias = jnp.zeros((1, LANES), F32)
    for hh in range(FOX_HEADS):
        colx = 8 * (hh // 2) + hh % 2
        gates = gates.at[:, colx].set(ff[:, hh])
        gbias = gbias.at[0, colx].set(fox_f_bias[hh])
    gates = gates.at[:, _GI:_GI + MLSTM_HEADS].set(mi).at[:, _GF:_GF + MLSTM_HEADS].set(mf)
    gbias = gbias.at[0, _GI:_GI + MLSTM_HEADS].set(mlstm_i_bias).at[0, _GF:_GF + MLSTM_HEADS].set(mlstm_f_bias)
    wt = jnp.concatenate([fk, sq, dup(sk), dup(sv), mq, mk, mv, mo, gates], axis=1).astype(BF16)
    wT = jnp.concatenate([fq, fv], axis=1).T.astype(BF16)
    fkg = jnp.tile(fox_k_norm * fox_q_norm * (HEAD_DIM ** -0.5 * LOG2E), FOX_HEADS)[None, :]
    sqg = jnp.tile(swa_q_norm * (HEAD_DIM ** -0.5), SWA_Q_HEADS)[None, :]
    skg = jnp.tile(swa_k_norm, 2 * SWA_KV_HEADS)[None, :]
    return wt, wT, fkg, sqg, skg, gbias, gl.astype(BF16)


def kernel(x, norm_mix, w_in, fox_f_bias, fox_q_norm, fox_k_norm, swa_q_norm, swa_k_norm, swa_sinks, conv_w,
           conv_b, mlstm_i_bias, mlstm_f_bias, mlstm_out_norm, w_branch, w_out, norm_mlp, w_up, w_down):
    B, S, D = x.shape
    depth = w_in.shape[0]
    tm, tq = _tiles(S)

    inv = ROPE_THETA ** (-jnp.arange(0, HEAD_DIM, 2, dtype=F32) / HEAD_DIM)
    ang = jnp.arange(S, dtype=F32)[:, None] * inv[None, :]
    cos, sin = jnp.cos(ang), jnp.sin(ang)
    cosf = jnp.tile(jnp.concatenate([cos, cos], axis=1), (1, LANES // HEAD_DIM))
    sinf = jnp.tile(jnp.concatenate([-sin, sin], axis=1), (1, LANES // HEAD_DIM))
    gm = jnp.asarray(np.kron(np.eye(MIX_WIDTH // HEAD_DIM), np.full((HEAD_DIM, HEAD_DIM), 1.0 / HEAD_DIM)), BF16)
    ltri = jnp.asarray(np.tril(np.ones((BLOCK, BLOCK))), BF16)
    shifts_np = np.zeros(((CONV_WIDTH - 1) * BLOCK, 2 * BLOCK))
    for j in range(CONV_WIDTH - 1):
        for t in range(BLOCK):
            shifts_np[j * BLOCK + t, BLOCK + t - (CONV_WIDTH - 1 - j)] = 1.0
    shifts = jnp.asarray(shifts_np, BF16)

    for l in range(depth):
        wt, wT, fkg, sqg, skg, gbias, wgl = _layer_params(
            w_in[l], fox_f_bias[l], fox_q_norm[l], fox_k_norm[l], swa_q_norm[l], swa_k_norm[l],
            mlstm_i_bias[l], mlstm_f_bias[l])
        (fqT, fvT, fk, sq, sk, sv, mq, mk, mv, og, gtok, gT) = _in_proj(
            x, norm_mix[l][None, :], wt, wT, fkg, sqg, skg, gm, cosf, sinf, gbias, ltri, tm=tm, tkv=tq)
        y_fox = _fox_attention(fqT, fk, fvT, gtok, tq=tq)
        y_swa = _swa_attention(swa_sinks[l], sq, sk, sv)
        y_ml = _mlstm(mq, mk, mv, og, gtok, gT, shifts, conv_w[l], conv_b[l][None, :], mlstm_out_norm[l][None, :])
        x2 = _merge(x.reshape(B * S, D), norm_mix[l][None, :], y_fox.reshape(B * S, -1), y_swa.reshape(B * S, -1),
                    y_ml.reshape(B * S, -1), wgl, w_branch[l].astype(BF16), w_out[l].astype(BF16), tm=tm)
        x2 = _mlp(x2, norm_mlp[l][None, :], w_up[l].astype(BF16), w_down[l].astype(BF16), tm=tm, fc=1024)
        x = x2.reshape(B, S, D)
    return x
```

```python
import functools

import jax
import jax.numpy as jnp
import numpy as np
from jax import lax
from jax.experimental import pallas as pl
from jax.experimental.pallas import tpu as pltpu

F32 = jnp.float32
BF16 = jnp.bfloat16

D_MODEL = 1024
HEAD_DIM = 64
MIX_WIDTH = 512
FOX_HEADS = 8
SWA_Q_HEADS = 8
SWA_KV_HEADS = 2
SWA_WINDOW = 128
MLSTM_HEADS = 4
MLSTM_HEAD_DIM = 128
CONV_WIDTH = 4
D_FF = 4 * D_MODEL
N_BRANCHES = 3
BLOCK = 128
ROPE_THETA = 10000.0
EPS = 1e-6
NEG = -1e30
LOG2E = 1.4426950408889634
FOX_GROUP = 2

LANES = 128
FOX_VROWS = LANES + 16
FOX_BIAS_PIECES = 3
VMEM_LIMIT_BYTES = 56 * 1024 * 1024

_FK, _SQ, _SK, _SV, _MQ, _MK, _MV, _MO, _GT, _NTOK = 0, 512, 1024, 1280, 1536, 2048, 2560, 3072, 3584, 3712
_GI, _GF = 32, 36

IN_SPLITS = (512, 512, 512, 8, 512, 128, 128, 512, 512, 512, 4, 4, 512, 3 * D_MODEL)


def _nt_dot(a, b):
    return lax.dot_general(a, b, (((1,), (1,)), ((), ())), preferred_element_type=F32)


def _tn_dot(a, b):
    return lax.dot_general(a, b, (((0,), (0,)), ((), ())), preferred_element_type=F32)


def _sigmoid(x):
    return 1.0 / (1.0 + jnp.exp(-x))


def _log_sigmoid(x):
    return jnp.minimum(x, 0.0) - jnp.log(1.0 + jnp.exp(-jnp.abs(x)))


def _rmsnorm_rows(x, gain_row):
    ms = jnp.mean(x * x, axis=-1, keepdims=True)
    return x * lax.rsqrt(ms + EPS) * gain_row


def _in_proj_kernel(x_ref, g_ref, wt_ref, wT_ref, fkg_ref, sqg_ref, skg_ref, gm_ref, cos_ref, sin_ref,
                    gbias_ref, ltri_ref, sel_ref,
                    fqT_ref, fvT_ref, fk_ref, sq_ref, sk_ref, sv_ref, mq_ref, mk_ref, mv_ref, og_ref,
                    gtok_ref, gT_ref, aug_ref, carry_ref, *, tm, tkv):
    s = pl.program_id(1)
    h = _rmsnorm_rows(x_ref[0], g_ref[...]).astype(BF16)

    def proj(lo, hi):
        return jnp.dot(h, wt_ref[:, lo:hi], preferred_element_type=F32)

    def head_ms(u):
        w = u.shape[1]
        return jnp.dot((u * u).astype(BF16), gm_ref[:w, :w], preferred_element_type=F32)

    def rope(u):
        lane = lax.broadcasted_iota(jnp.int32, (tm, LANES), 1)
        first = (lane % HEAD_DIM) < (HEAD_DIM // 2)
        c, sn = cos_ref[...], sin_ref[...]
        outs = []
        for j in range(u.shape[1] // LANES):
            uj = u[:, j * LANES:(j + 1) * LANES]
            partner = jnp.where(first, pltpu.roll(uj, LANES - HEAD_DIM // 2, 1), pltpu.roll(uj, HEAD_DIM // 2, 1))
            outs.append(uj * c + partner * sn)
        return jnp.concatenate(outs, axis=1)

    qT = _nt_dot(wT_ref[0:MIX_WIDTH, :], h)
    msT = jnp.dot(gm_ref[...], (qT * qT).astype(BF16), preferred_element_type=F32)
    qn = (qT * lax.rsqrt(msT + EPS)).astype(BF16)
    pad = jnp.zeros((HEAD_DIM, tm), BF16)
    for hd in range(FOX_HEADS):
        rows = qn[hd * HEAD_DIM:(hd + 1) * HEAD_DIM]
        fqT_ref[0, hd] = jnp.concatenate([rows, pad] if hd % 2 == 0 else [pad, rows], axis=0)
    vT = _nt_dot(wT_ref[MIX_WIDTH:2 * MIX_WIDTH, :], h).astype(BF16)
    ones = jnp.ones((FOX_VROWS - LANES, tkv), BF16)
    for c in range(tm // tkv):
        for pr in range(FOX_HEADS // 2):
            fvT_ref[0, c, pr] = jnp.concatenate([vT[pr * LANES:(pr + 1) * LANES, c * tkv:(c + 1) * tkv], ones], axis=0)

    u = proj(_FK, _SQ)
    kn = (u * lax.rsqrt(head_ms(u) + EPS) * fkg_ref[...]).astype(BF16)
    for pr in range(FOX_HEADS // 2):
        fk_ref[0, pr] = kn[:, pr * LANES:(pr + 1) * LANES]

    u = proj(_SQ, _SK)
    sq_ref[0] = rope(u * lax.rsqrt(head_ms(u) + EPS) * sqg_ref[...]).astype(BF16)
    u = proj(_SK, _SV)
    sk_ref[0] = rope(u * lax.rsqrt(head_ms(u) + EPS) * skg_ref[...]).astype(BF16)
    sv_ref[0] = proj(_SV, _MQ).astype(BF16)

    mq_ref[0] = proj(_MQ, _MK).astype(BF16)
    mk_ref[0] = proj(_MK, _MV).astype(BF16)
    mv_ref[0] = proj(_MV, _MO).astype(BF16)
    og_ref[0] = _sigmoid(proj(_MO, _GT)).astype(BF16)

    @pl.when(s == 0)
    def _():
        carry_ref[...] = jnp.zeros_like(carry_ref)

    g = proj(_GT, _NTOK) + gbias_ref[...]
    col = lax.broadcasted_iota(jnp.int32, (BLOCK, LANES), 1)
    is_i = (col >= _GI) & (col < _GF)
    is_fox = col < _GI
    ltri = ltri_ref[...]
    for c in range(tm // BLOCK):
        gc = g[c * BLOCK:(c + 1) * BLOCK]
        val = _log_sigmoid(gc)
        hi = val.astype(BF16)
        r1 = val - hi.astype(F32)
        mid = r1.astype(BF16)
        lo = (r1 - mid.astype(F32)).astype(BF16)
        cs = (jnp.dot(ltri, hi, preferred_element_type=F32) + jnp.dot(ltri, mid, preferred_element_type=F32)
              + jnp.dot(ltri, lo, preferred_element_type=F32))
        cs = cs + jnp.where(is_fox, carry_ref[...], 0.0)
        out_c = jnp.where(is_i, gc, cs)
        carry_ref[...] = out_c[BLOCK - 1:BLOCK, :]
        gtok_ref[0, c * BLOCK:(c + 1) * BLOCK, :] = out_c
        gT_ref[0, c] = out_c.T
        rest = out_c * (-LOG2E)
        pieces = []
        for _ in range(FOX_BIAS_PIECES):
            part = rest.astype(BF16)
            pieces.append(part)
            rest = rest - part.astype(F32)
        aug = jnp.dot(jnp.concatenate(pieces, axis=1), sel_ref[...], preferred_element_type=F32).astype(BF16)
        for pr in range(FOX_HEADS // 2):
            aug_ref[0, pr, c * BLOCK:(c + 1) * BLOCK, :] = aug[:, pr * LANES:(pr + 1) * LANES]


def _in_proj(x, gain, wt, wT, fkg, sqg, skg, gm, cosf, sinf, gbias, ltri, sel, *, tm, tkv):
    B, S, D = x.shape
    nc = S // BLOCK
    const = lambda shape: pl.BlockSpec(shape, lambda b, s: (0,) * len(shape))
    tok = lambda w: pl.BlockSpec((1, tm, w), lambda b, s: (b, s, 0))
    out_shape = (
        jax.ShapeDtypeStruct((B, FOX_HEADS, LANES, S), BF16),
        jax.ShapeDtypeStruct((B, S // tkv, FOX_HEADS // 2, FOX_VROWS, tkv), BF16),
        jax.ShapeDtypeStruct((B, FOX_HEADS // 2, S, LANES), BF16),
        jax.ShapeDtypeStruct((B, S, 512), BF16),
        jax.ShapeDtypeStruct((B, S, 256), BF16),
        jax.ShapeDtypeStruct((B, S, 256), BF16),
        jax.ShapeDtypeStruct((B, S, 512), BF16),
        jax.ShapeDtypeStruct((B, S, 512), BF16),
        jax.ShapeDtypeStruct((B, S, 512), BF16),
        jax.ShapeDtypeStruct((B, S, 512), BF16),
        jax.ShapeDtypeStruct((B, S, LANES), F32),
        jax.ShapeDtypeStruct((B, nc, LANES, BLOCK), F32),
        jax.ShapeDtypeStruct((B, FOX_HEADS // 2, S, LANES), BF16),
    )
    out_specs = (
        pl.BlockSpec((1, FOX_HEADS, LANES, tm), lambda b, s: (b, 0, 0, s)),
        pl.BlockSpec((1, tm // tkv, FOX_HEADS // 2, FOX_VROWS, tkv), lambda b, s: (b, s, 0, 0, 0)),
        pl.BlockSpec((1, FOX_HEADS // 2, tm, LANES), lambda b, s: (b, 0, s, 0)),
        tok(512), tok(256), tok(256), tok(512), tok(512), tok(512), tok(512), tok(LANES),
        pl.BlockSpec((1, tm // BLOCK, LANES, BLOCK), lambda b, s: (b, s, 0, 0)),
        pl.BlockSpec((1, FOX_HEADS // 2, tm, LANES), lambda b, s: (b, 0, s, 0)),
    )
    in_specs = [
        pl.BlockSpec((1, tm, D), lambda b, s: (b, s, 0)),
        const((1, D)), const(wt.shape), const(wT.shape), const((1, 512)), const((1, 512)), const((1, 256)),
        const((512, 512)),
        pl.BlockSpec((tm, LANES), lambda b, s: (s, 0)), pl.BlockSpec((tm, LANES), lambda b, s: (s, 0)),
        const((1, LANES)), const((BLOCK, BLOCK)), const(sel.shape),
    ]
    return pl.pallas_call(
        functools.partial(_in_proj_kernel, tm=tm, tkv=tkv),
        grid=(B, S // tm), in_specs=in_specs, out_specs=out_specs, out_shape=out_shape,
        scratch_shapes=[pltpu.VMEM((1, LANES), F32)],
        compiler_params=pltpu.CompilerParams(dimension_semantics=("parallel", "arbitrary"),
                                             vmem_limit_bytes=VMEM_LIMIT_BYTES),
        name="in_proj",
    )(x, gain, wt, wT, fkg, sqg, skg, gm, cosf, sinf, gbias, ltri, sel)


def _col_reduce(x, op, ways=8):
    rows, cols = x.shape
    return op(op(x.reshape(ways, rows // ways, cols), axis=0), axis=0, keepdims=True)


def _fox_kernel(qT_ref, k_ref, vT_ref, aug_ref, o_ref, qq_ref, m_ref, l_ref, acc_ref, za_ref, zb_ref, p_ref,
                mt_ref, al1_ref, al2_ref, *, tq):
    i = pl.program_id(1)
    NP = FOX_HEADS // 2
    G = FOX_GROUP
    NAUG = FOX_BIAS_PIECES

    row = lax.broadcasted_iota(jnp.int32, (LANES, 2 * tq), 0)
    col = lax.broadcasted_iota(jnp.int32, (LANES, 2 * tq), 1)
    ones_rows = jnp.where((row < 2 * NAUG) & ((row >= NAUG) == (col >= tq)), 1.0, 0.0).astype(BF16)
    for pr in range(NP):
        qq_ref[pr] = jnp.concatenate(
            [jnp.concatenate([qT_ref[0, 2 * pr], qT_ref[0, 2 * pr + 1]], axis=1), ones_rows], axis=0)
    m_ref[...] = jnp.full(m_ref.shape, NEG, F32)
    l_ref[...] = jnp.zeros(l_ref.shape, F32)
    acc_ref[...] = jnp.zeros(acc_ref.shape, F32)
    za_ref[...] = jnp.full(za_ref.shape, NEG, F32)
    zb_ref[...] = jnp.full(zb_ref.shape, NEG, F32)
    p_ref[...] = jnp.zeros(p_ref.shape, BF16)
    mt_ref[...] = jnp.full(mt_ref.shape, NEG, F32)
    al1_ref[...] = jnp.ones(al1_ref.shape, F32)
    al2_ref[...] = jnp.ones(al2_ref.shape, F32)
    krow = lax.broadcasted_iota(jnp.int32, (tq, 2 * tq), 0)
    qcol = lax.broadcasted_iota(jnp.int32, (tq, 2 * tq), 1)
    causal = krow <= (qcol & (tq - 1))
    n_end = NP * (i + 1)

    def where_is(n):
        tile = jnp.clip(n >> 2, 0, i)
        return n & (NP - 1), tile, pl.multiple_of(tile * tq, tq)

    def stage_scores(n, j, guard):
        pr, _, r0 = where_is(n)
        kk = jnp.concatenate([k_ref[0, pr, pl.ds(r0, tq), :], aug_ref[0, pr, pl.ds(r0, tq), :]], axis=1)
        s = jnp.dot(kk, qq_ref[pr], preferred_element_type=F32)
        za_ref[j] = jnp.where(n < n_end, s, NEG) if guard else s

    def stage_max(n, j, masked):
        pr, _, _ = where_is(n)
        z = za_ref[j]
        if masked:
            z = jnp.where(causal, z, NEG)
        zb_ref[j] = z
        m = m_ref[pr]
        m_new = jnp.maximum(m, _col_reduce(z, jnp.max))
        al1_ref[j] = jnp.exp2(m - m_new)
        mt_ref[j] = m_new
        m_ref[pr] = m_new

    def stage_probs(n, j):
        p_ref[j] = jnp.exp2(zb_ref[j] - mt_ref[j]).astype(BF16)
        al2_ref[j] = al1_ref[j]

    def stage_values(n, j):
        pr, tile, _ = where_is(n)
        pv = jnp.dot(vT_ref[0, tile, pr], p_ref[j], preferred_element_type=F32)
        al = al2_ref[j]
        acc_ref[2 * pr] = al[:, :tq] * acc_ref[2 * pr] + pv[:HEAD_DIM, :tq]
        acc_ref[2 * pr + 1] = al[:, tq:] * acc_ref[2 * pr + 1] + pv[HEAD_DIM:2 * HEAD_DIM, tq:]
        l_ref[pr] = al * l_ref[pr] + pv[2 * HEAD_DIM:2 * HEAD_DIM + 1, :]

    def step(g, masked, guard):
        for j in range(G):
            stage_values((g - 3) * G + j, j)
        for j in range(G):
            stage_probs((g - 2) * G + j, j)
        for j in range(G):
            stage_max((g - 1) * G + j, j, masked)
        for j in range(G):
            stage_scores(g * G + j, j, guard)

    def body_plain(g, carry):
        step(g, False, False)
        return carry

    def body_diag(g, carry):
        step(g, True, True)
        return carry

    g_diag = (NP // G) * i + 1
    lax.fori_loop(0, g_diag, body_plain, 0)
    lax.fori_loop(g_diag, (NP // G) * (i + 1) + 3, body_diag, 0)
    for pr in range(NP):
        la, lb = l_ref[pr][:, :tq], l_ref[pr][:, tq:]
        oT = jnp.concatenate([acc_ref[2 * pr] / la, acc_ref[2 * pr + 1] / lb], axis=0)
        o_ref[0, :, pr * LANES:(pr + 1) * LANES] = oT.T.astype(BF16)


def _fox_attention(fqT, fk, fvT, faug, *, tq):
    B, NP, S, _ = fk.shape
    nkv = S // tq
    vrows = fvT.shape[3]
    return pl.pallas_call(
        functools.partial(_fox_kernel, tq=tq),
        grid=(B, S // tq),
        in_specs=[
            pl.BlockSpec((1, FOX_HEADS, LANES, tq), lambda b, i: (b, 0, 0, i)),
            pl.BlockSpec((1, NP, S, LANES), lambda b, i: (b, 0, 0, 0)),
            pl.BlockSpec((1, nkv, NP, vrows, tq), lambda b, i: (b, 0, 0, 0, 0)),
            pl.BlockSpec((1, NP, S, LANES), lambda b, i: (b, 0, 0, 0)),
        ],
        out_specs=pl.BlockSpec((1, tq, MIX_WIDTH), lambda b, i: (b, i, 0)),
        out_shape=jax.ShapeDtypeStruct((B, S, MIX_WIDTH), BF16),
        scratch_shapes=[
            pltpu.VMEM((NP, 2 * LANES, 2 * tq), BF16),
            pltpu.VMEM((NP, 1, 2 * tq), F32),
            pltpu.VMEM((NP, 1, 2 * tq), F32),
            pltpu.VMEM((FOX_HEADS, HEAD_DIM, tq), F32),
            pltpu.VMEM((FOX_GROUP, tq, 2 * tq), F32),
            pltpu.VMEM((FOX_GROUP, tq, 2 * tq), F32),
            pltpu.VMEM((FOX_GROUP, tq, 2 * tq), BF16),
            pltpu.VMEM((FOX_GROUP, 1, 2 * tq), F32),
            pltpu.VMEM((FOX_GROUP, 1, 2 * tq), F32),
            pltpu.VMEM((FOX_GROUP, 1, 2 * tq), F32),
        ],
        compiler_params=pltpu.CompilerParams(dimension_semantics=("parallel", "arbitrary"),
                                             vmem_limit_bytes=VMEM_LIMIT_BYTES),
        name="fox_attention",
    )(fqT, fk, fvT, faug)


def _swa_kernel(sinks_ref, q_ref, kp_ref, kc_ref, vp_ref, vc_ref, o_ref):
    n = pl.program_id(1)
    L = BLOCK
    row = lax.broadcasted_iota(jnp.int32, (L, 2 * L), 0)
    col = lax.broadcasted_iota(jnp.int32, (L, 2 * L), 1)
    row_prev = row + jnp.where(n == 0, 2 * L, 0)
    mask = ((col < L) & (col > row_prev)) | ((col >= L) & ((col - L) <= row))
    lane = lax.broadcasted_iota(jnp.int32, (L, LANES), 1)
    low = lane < HEAD_DIM
    q = q_ref[0]
    zero = jnp.zeros((L, LANES), BF16)
    for pair in range(SWA_Q_HEADS // 2):
        g = pair // 2
        kf = jnp.concatenate([kp_ref[0, :, g * LANES:(g + 1) * LANES], kc_ref[0, :, g * LANES:(g + 1) * LANES]], axis=0)
        vf = jnp.concatenate([vp_ref[0, :, g * LANES:(g + 1) * LANES], vc_ref[0, :, g * LANES:(g + 1) * LANES]], axis=0)
        qp = q[:, pair * LANES:(pair + 1) * LANES]
        outs = []
        for half in range(2):
            sink = sinks_ref[2 * pair + half]
            qh = jnp.where(low if half == 0 else ~low, qp, zero)
            z = jnp.where(mask, _nt_dot(qh, kf), NEG)
            m = jnp.maximum(jnp.max(z, axis=1, keepdims=True), sink)
            pm = jnp.exp(z - m)
            den = jnp.sum(pm, axis=1, keepdims=True) + jnp.exp(sink - m)
            outs.append(jnp.dot(pm.astype(BF16), vf, preferred_element_type=F32) / den)
        o_ref[0, :, pair * LANES:(pair + 1) * LANES] = jnp.where(low, outs[0], outs[1]).astype(BF16)


def _swa_attention(sinks, sq, sk, sv):
    B, S, _ = sq.shape
    prev = lambda b, n: (b, jnp.maximum(n - 1, 0), 0)
    cur = lambda b, n: (b, n, 0)
    return pl.pallas_call(
        _swa_kernel,
        grid=(B, S // BLOCK),
        in_specs=[
            pl.BlockSpec(memory_space=pltpu.SMEM),
            pl.BlockSpec((1, BLOCK, 512), cur),
            pl.BlockSpec((1, BLOCK, 256), prev), pl.BlockSpec((1, BLOCK, 256), cur),
            pl.BlockSpec((1, BLOCK, 256), prev), pl.BlockSpec((1, BLOCK, 256), cur),
        ],
        out_specs=pl.BlockSpec((1, BLOCK, 512), cur),
        out_shape=jax.ShapeDtypeStruct((B, S, MIX_WIDTH), BF16),
        compiler_params=pltpu.CompilerParams(dimension_semantics=("parallel", "parallel"),
                                             vmem_limit_bytes=VMEM_LIMIT_BYTES),
        name="swa_attention",
    )(sinks, sq, sk, sk, sv, sv)


def _mlstm_kernel(qp_ref, q_ref, kp_ref, k_ref, v_ref, og_ref, gt_ref, gT_ref, sh_ref, cw_ref, cb_ref, on_ref,
                  y_ref, c_ref, n_ref, m_ref):
    c = pl.program_id(1)
    L = BLOCK

    @pl.when(c == 0)
    def _():
        c_ref[...] = jnp.zeros_like(c_ref)
        n_ref[...] = jnp.zeros_like(n_ref)
        m_ref[...] = jnp.zeros_like(m_ref)

    def conv_silu(p_ref, x_ref, off):
        cur = x_ref[0]
        prev = jnp.where(c > 0, p_ref[0], jnp.zeros_like(cur))
        sh = jnp.dot(sh_ref[...], jnp.concatenate([prev, cur], axis=0), preferred_element_type=F32)
        y = cb_ref[:, off:off + MIX_WIDTH]
        for j in range(CONV_WIDTH - 1):
            y = y + sh[j * L:(j + 1) * L] * cw_ref[j:j + 1, off:off + MIX_WIDTH]
        y = y + cur.astype(F32) * cw_ref[CONV_WIDTH - 1:CONV_WIDTH, off:off + MIX_WIDTH]
        return y * _sigmoid(y)

    qc = conv_silu(qp_ref, q_ref, 0)
    kc = conv_silu(kp_ref, k_ref, MIX_WIDTH) * (MLSTM_HEAD_DIM ** -0.5)
    gt = gt_ref[0]
    gT = gT_ref[0, 0]
    tri = lax.broadcasted_iota(jnp.int32, (L, L), 0) >= lax.broadcasted_iota(jnp.int32, (L, L), 1)

    for hd in range(MLSTM_HEADS):
        sl = slice(hd * MLSTM_HEAD_DIM, (hd + 1) * MLSTM_HEAD_DIM)
        q = qc[:, sl]
        k = kc[:, sl]
        qb, kb = q.astype(BF16), k.astype(BF16)
        v = v_ref[0, :, sl]
        i_col, b_col = gt[:, _GI + hd:_GI + hd + 1], gt[:, _GF + hd:_GF + hd + 1]
        i_row, b_row = gT[_GI + hd:_GI + hd + 1, :], gT[_GF + hd:_GF + hd + 1, :]
        b_last = b_col[L - 1:L, :]

        dmat = jnp.where(tri, b_col - b_row + i_row, NEG)
        m_loc = jnp.max(dmat, axis=1, keepdims=True)
        smat = jnp.exp(dmat - m_loc) * _nt_dot(qb, kb)
        s_v = jnp.dot(smat.astype(BF16), v, preferred_element_type=F32)
        s_sum = jnp.sum(smat, axis=1, keepdims=True)
        g_col = b_last - b_col + i_col
        g_max = jnp.max(b_last - b_row + i_row, axis=1, keepdims=True)
        kw = jnp.exp(g_col - g_max) * k
        kv_new = _tn_dot(kw.astype(BF16), v)
        k_new = jnp.sum(kw, axis=0, keepdims=True)

        m_prev = m_ref[hd]
        cmat = c_ref[hd]
        n_row = n_ref[hd]
        a_col = b_col + m_prev
        mt = jnp.maximum(a_col, m_loc)
        w_inter = jnp.exp(a_col - mt)
        w_intra = jnp.exp(m_loc - mt)
        num = w_inter * jnp.dot(qb, cmat.astype(BF16), preferred_element_type=F32) + w_intra * s_v
        den = w_inter * jnp.sum(q * n_row, axis=1, keepdims=True) + w_intra * s_sum
        hcur = num / jnp.maximum(jnp.abs(den), jnp.exp(-mt))

        m_new = jnp.maximum(b_last + m_prev, g_max)
        decay = jnp.exp(b_last + m_prev - m_new)
        w_new = jnp.exp(g_max - m_new)
        c_ref[hd] = decay * cmat + w_new * kv_new
        n_ref[hd] = decay * n_row + w_new * k_new
        m_ref[hd] = m_new

        hn = _rmsnorm_rows(hcur, on_ref[:, sl])
        y_ref[0, :, sl] = (hn * og_ref[0, :, sl].astype(F32)).astype(BF16)


def _mlstm(mq, mk, mv, og, gtok, gT, shifts, conv_w, conv_b, out_norm):
    B, S, _ = mq.shape
    cur = lambda b, c: (b, c, 0)
    prev = lambda b, c: (b, jnp.maximum(c - 1, 0), 0)
    tok = lambda w: pl.BlockSpec((1, BLOCK, w), cur)
    const = lambda shape: pl.BlockSpec(shape, lambda b, c: (0,) * len(shape))
    return pl.pallas_call(
        _mlstm_kernel,
        grid=(B, S // BLOCK),
        in_specs=[pl.BlockSpec((1, BLOCK, 512), prev), tok(512), pl.BlockSpec((1, BLOCK, 512), prev), tok(512),
                  tok(512), tok(512), tok(LANES),
                  pl.BlockSpec((1, 1, LANES, BLOCK), lambda b, c: (b, c, 0, 0)),
                  const(shifts.shape), const((CONV_WIDTH, 2 * MIX_WIDTH)), const((1, 2 * MIX_WIDTH)),
                  const((1, MIX_WIDTH))],
        out_specs=tok(512),
        out_shape=jax.ShapeDtypeStruct((B, S, MIX_WIDTH), BF16),
        scratch_shapes=[
            pltpu.VMEM((MLSTM_HEADS, MLSTM_HEAD_DIM, MLSTM_HEAD_DIM), F32),
            pltpu.VMEM((MLSTM_HEADS, 1, MLSTM_HEAD_DIM), F32),
            pltpu.VMEM((MLSTM_HEADS, 1, 1), F32),
        ],
        compiler_params=pltpu.CompilerParams(dimension_semantics=("parallel", "arbitrary"),
                                             vmem_limit_bytes=VMEM_LIMIT_BYTES),
        name="mlstm",
    )(mq, mq, mk, mk, mv, og, gtok, gT, shifts, conv_w, conv_b, out_norm)


def _merge_kernel(x_ref, g_ref, yf_ref, ys_ref, ym_ref, wgl_ref, wb_ref, wo_ref, o_ref):
    x = x_ref[...]
    h = _rmsnorm_rows(x, g_ref[...]).astype(BF16)
    merged = None
    for br, y_ref in enumerate((yf_ref, ys_ref, ym_ref)):
        gate = _sigmoid(jnp.dot(h, wgl_ref[:, br * D_MODEL:(br + 1) * D_MODEL], preferred_element_type=F32))
        term = gate * jnp.dot(y_ref[...], wb_ref[br], preferred_element_type=F32)
        merged = term if merged is None else merged + term
    o_ref[...] = x + jnp.dot(merged.astype(BF16), wo_ref[...], preferred_element_type=F32)


def _merge(x2, gain, yf, ys, ym, wgl, wb, wo, *, tm):
    T, D = x2.shape
    row = lambda w: pl.BlockSpec((tm, w), lambda t: (t, 0))
    const = lambda shape: pl.BlockSpec(shape, lambda t: (0,) * len(shape))
    return pl.pallas_call(
        _merge_kernel,
        grid=(T // tm,),
        in_specs=[row(D), const((1, D)), row(512), row(512), row(512),
                  const(wgl.shape), const(wb.shape), const(wo.shape)],
        out_specs=row(D),
        out_shape=jax.ShapeDtypeStruct((T, D), F32),
        compiler_params=pltpu.CompilerParams(dimension_semantics=("parallel",),
                                             vmem_limit_bytes=VMEM_LIMIT_BYTES),
        name="merge_out",
    )(x2, gain, yf, ys, ym, wgl, wb, wo)


def _mlp_kernel(x_ref, g_ref, wu_ref, wd_ref, o_ref, *, fc):
    x = x_ref[...]
    h = _rmsnorm_rows(x, g_ref[...]).astype(BF16)
    acc = x
    for f in range(D_FF // fc):
        a = jnp.maximum(jnp.dot(h, wu_ref[:, f * fc:(f + 1) * fc], preferred_element_type=F32), 0.0)
        acc = acc + jnp.dot((a * a).astype(BF16), wd_ref[f * fc:(f + 1) * fc, :], preferred_element_type=F32)
    o_ref[...] = acc


def _mlp(x2, gain, wu, wd, *, tm, fc):
    T, D = x2.shape
    row = pl.BlockSpec((tm, D), lambda t: (t, 0))
    const = lambda shape: pl.BlockSpec(shape, lambda t: (0,) * len(shape))
    return pl.pallas_call(
        functools.partial(_mlp_kernel, fc=fc),
        grid=(T // tm,),
        in_specs=[row, const((1, D)), const(wu.shape), const(wd.shape)],
        out_specs=row,
        out_shape=jax.ShapeDtypeStruct((T, D), F32),
        compiler_params=pltpu.CompilerParams(dimension_semantics=("parallel",),
                                             vmem_limit_bytes=VMEM_LIMIT_BYTES),
        name="mlp",
    )(x2, gain, wu, wd)


def _tiles(S):
    tm = 512 if S % 512 == 0 else S
    tq = 256 if S % 256 == 0 else S
    return tm, tq


def _layer_params(w_in, fox_f_bias, fox_q_norm, fox_k_norm, swa_q_norm, swa_k_norm, mlstm_i_bias, mlstm_f_bias):
    offs = np.cumsum(IN_SPLITS)[:-1].tolist()
    fq, fk, fv, ff, sq, sk, sv, mq, mk, mv, mi, mf, mo, gl = jnp.split(w_in, offs, axis=1)
    dup = lambda w: jnp.concatenate([w[:, :HEAD_DIM], w[:, :HEAD_DIM], w[:, HEAD_DIM:], w[:, HEAD_DIM:]], axis=1)
    gates = jnp.zeros((D_MODEL, LANES), F32)
    gbias = jnp.zeros((1, LANES), F32)
    for hh in range(FOX_HEADS):
        colx = 8 * (hh // 2) + hh % 2
        gates = gates.at[:, colx].set(ff[:, hh])
        gbias = gbias.at[0, colx].set(fox_f_bias[hh])
    gates = gates.at[:, _GI:_GI + MLSTM_HEADS].set(mi).at[:, _GF:_GF + MLSTM_HEADS].set(mf)
    gbias = gbias.at[0, _GI:_GI + MLSTM_HEADS].set(mlstm_i_bias).at[0, _GF:_GF + MLSTM_HEADS].set(mlstm_f_bias)
    wt = jnp.concatenate([fk, sq, dup(sk), dup(sv), mq, mk, mv, mo, gates], axis=1).astype(BF16)
    wT = jnp.concatenate([fq, fv], axis=1).T.astype(BF16)
    fkg = jnp.tile(fox_k_norm * fox_q_norm * (HEAD_DIM ** -0.5 * LOG2E), FOX_HEADS)[None, :]
    sqg = jnp.tile(swa_q_norm * (HEAD_DIM ** -0.5), SWA_Q_HEADS)[None, :]
    skg = jnp.tile(swa_k_norm, 2 * SWA_KV_HEADS)[None, :]
    return wt, wT, fkg, sqg, skg, gbias, gl.astype(BF16)


def kernel(x, norm_mix, w_in, fox_f_bias, fox_q_norm, fox_k_norm, swa_q_norm, swa_k_norm, swa_sinks, conv_w,
           conv_b, mlstm_i_bias, mlstm_f_bias, mlstm_out_norm, w_branch, w_out, norm_mlp, w_up, w_down):
    B, S, D = x.shape
    depth = w_in.shape[0]
    tm, tq = _tiles(S)

    inv = ROPE_THETA ** (-jnp.arange(0, HEAD_DIM, 2, dtype=F32) / HEAD_DIM)
    ang = jnp.arange(S, dtype=F32)[:, None] * inv[None, :]
    cos, sin = jnp.cos(ang), jnp.sin(ang)
    cosf = jnp.tile(jnp.concatenate([cos, cos], axis=1), (1, LANES // HEAD_DIM))
    sinf = jnp.tile(jnp.concatenate([-sin, sin], axis=1), (1, LANES // HEAD_DIM))
    gm = jnp.asarray(np.kron(np.eye(MIX_WIDTH // HEAD_DIM), np.full((HEAD_DIM, HEAD_DIM), 1.0 / HEAD_DIM)), BF16)
    ltri = jnp.asarray(np.tril(np.ones((BLOCK, BLOCK))), BF16)
    shifts_np = np.zeros(((CONV_WIDTH - 1) * BLOCK, 2 * BLOCK))
    for j in range(CONV_WIDTH - 1):
        for t in range(BLOCK):
            shifts_np[j * BLOCK + t, BLOCK + t - (CONV_WIDTH - 1 - j)] = 1.0
    shifts = jnp.asarray(shifts_np, BF16)
    sel_np = np.zeros((FOX_BIAS_PIECES * LANES, MIX_WIDTH))
    for piece in range(FOX_BIAS_PIECES):
        for hd in range(FOX_HEADS):
            sel_np[piece * LANES + 8 * (hd // 2) + hd % 2, LANES * (hd // 2) + FOX_BIAS_PIECES * (hd % 2) + piece] = 1.0
    sel = jnp.asarray(sel_np, BF16)

    for l in range(depth):
        wt, wT, fkg, sqg, skg, gbias, wgl = _layer_params(
            w_in[l], fox_f_bias[l], fox_q_norm[l], fox_k_norm[l], swa_q_norm[l], swa_k_norm[l],
            mlstm_i_bias[l], mlstm_f_bias[l])
        (fqT, fvT, fk, sq, sk, sv, mq, mk, mv, og, gtok, gT, faug) = _in_proj(
            x, norm_mix[l][None, :], wt, wT, fkg, sqg, skg, gm, cosf, sinf, gbias, ltri, sel, tm=tm, tkv=tq)
        y_fox = _fox_attention(fqT, fk, fvT, faug, tq=tq)
        y_swa = _swa_attention(swa_sinks[l], sq, sk, sv)
        y_ml = _mlstm(mq, mk, mv, og, gtok, gT, shifts, conv_w[l], conv_b[l][None, :], mlstm_out_norm[l][None, :])
        x2 = _merge(x.reshape(B * S, D), norm_mix[l][None, :], y_fox.reshape(B * S, -1), y_swa.reshape(B * S, -1),
                    y_ml.reshape(B * S, -1), wgl, w_branch[l].astype(BF16), w_out[l].astype(BF16), tm=tm)
        x2 = _mlp(x2, norm_mlp[l][None, :], w_up[l].astype(BF16), w_down[l].astype(BF16), tm=tm, fc=1024)
        x = x2.reshape(B, S, D)
    return x
```

```python
import functools

import jax
import jax.numpy as jnp
import numpy as np
from jax import lax
from jax.experimental import pallas as pl
from jax.experimental.pallas import tpu as pltpu

F32 = jnp.float32
BF16 = jnp.bfloat16

D_MODEL = 1024
HEAD_DIM = 64
MIX_WIDTH = 512
FOX_HEADS = 8
SWA_Q_HEADS = 8
SWA_KV_HEADS = 2
SWA_WINDOW = 128
MLSTM_HEADS = 4
MLSTM_HEAD_DIM = 128
CONV_WIDTH = 4
D_FF = 4 * D_MODEL
N_BRANCHES = 3
BLOCK = 128
ROPE_THETA = 10000.0
EPS = 1e-6
NEG = -1e30
LOG2E = 1.4426950408889634
FOX_GROUP = 2

LANES = 128
FOX_VROWS = LANES + 16
FOX_BIAS_PIECES = 3
VMEM_LIMIT_BYTES = 56 * 1024 * 1024

_FK, _SQ, _SK, _SV, _MQ, _MK, _MV, _MO, _GT, _NTOK = 0, 512, 1024, 1280, 1536, 2048, 2560, 3072, 3584, 3712
_GI, _GF = 32, 36

IN_SPLITS = (512, 512, 512, 8, 512, 128, 128, 512, 512, 512, 4, 4, 512, 3 * D_MODEL)


def _nt_dot(a, b):
    return lax.dot_general(a, b, (((1,), (1,)), ((), ())), preferred_element_type=F32)


def _tn_dot(a, b):
    return lax.dot_general(a, b, (((0,), (0,)), ((), ())), preferred_element_type=F32)


def _sigmoid(x):
    return 1.0 / (1.0 + jnp.exp(-x))


def _log_sigmoid(x):
    return jnp.minimum(x, 0.0) - jnp.log(1.0 + jnp.exp(-jnp.abs(x)))


def _rmsnorm_rows(x, gain_row):
    ms = jnp.mean(x * x, axis=-1, keepdims=True)
    return x * lax.rsqrt(ms + EPS) * gain_row


def _in_proj_kernel(x_ref, g_ref, wt_ref, wT_ref, fkg_ref, sqg_ref, skg_ref, gm_ref, cos_ref, sin_ref,
                    gbias_ref, ltri_ref, sel_ref,
                    fqT_ref, fvT_ref, fk_ref, sq_ref, sk_ref, sv_ref, mq_ref, mk_ref, mv_ref, og_ref,
                    gtok_ref, gT_ref, aug_ref, carry_ref, *, tm, tkv):
    s = pl.program_id(1)
    h = _rmsnorm_rows(x_ref[0], g_ref[...]).astype(BF16)

    def proj(lo, hi):
        return jnp.dot(h, wt_ref[:, lo:hi], preferred_element_type=F32)

    def head_ms(u):
        w = u.shape[1]
        return jnp.dot((u * u).astype(BF16), gm_ref[:w, :w], preferred_element_type=F32)

    def rope(u):
        lane = lax.broadcasted_iota(jnp.int32, (tm, LANES), 1)
        first = (lane % HEAD_DIM) < (HEAD_DIM // 2)
        c, sn = cos_ref[...], sin_ref[...]
        outs = []
        for j in range(u.shape[1] // LANES):
            uj = u[:, j * LANES:(j + 1) * LANES]
            partner = jnp.where(first, pltpu.roll(uj, LANES - HEAD_DIM // 2, 1), pltpu.roll(uj, HEAD_DIM // 2, 1))
            outs.append(uj * c + partner * sn)
        return jnp.concatenate(outs, axis=1)

    qT = _nt_dot(wT_ref[0:MIX_WIDTH, :], h)
    msT = jnp.dot(gm_ref[...], (qT * qT).astype(BF16), preferred_element_type=F32)
    qn = (qT * lax.rsqrt(msT + EPS)).astype(BF16)
    pad = jnp.zeros((HEAD_DIM, tm), BF16)
    for hd in range(FOX_HEADS):
        rows = qn[hd * HEAD_DIM:(hd + 1) * HEAD_DIM]
        fqT_ref[0, hd] = jnp.concatenate([rows, pad] if hd % 2 == 0 else [pad, rows], axis=0)
    vT = _nt_dot(wT_ref[MIX_WIDTH:2 * MIX_WIDTH, :], h).astype(BF16)
    ones = jnp.ones((FOX_VROWS - LANES, tkv), BF16)
    for c in range(tm // tkv):
        for pr in range(FOX_HEADS // 2):
            fvT_ref[0, c, pr] = jnp.concatenate([vT[pr * LANES:(pr + 1) * LANES, c * tkv:(c + 1) * tkv], ones], axis=0)

    u = proj(_FK, _SQ)
    kn = (u * lax.rsqrt(head_ms(u) + EPS) * fkg_ref[...]).astype(BF16)
    for pr in range(FOX_HEADS // 2):
        fk_ref[0, pr] = kn[:, pr * LANES:(pr + 1) * LANES]

    u = proj(_SQ, _SK)
    sq_ref[0] = rope(u * lax.rsqrt(head_ms(u) + EPS) * sqg_ref[...]).astype(BF16)
    u = proj(_SK, _SV)
    sk_ref[0] = rope(u * lax.rsqrt(head_ms(u) + EPS) * skg_ref[...]).astype(BF16)
    sv_ref[0] = proj(_SV, _MQ).astype(BF16)

    mq_ref[0] = proj(_MQ, _MK).astype(BF16)
    mk_ref[0] = proj(_MK, _MV).astype(BF16)
    mv_ref[0] = proj(_MV, _MO).astype(BF16)
    og_ref[0] = _sigmoid(proj(_MO, _GT)).astype(BF16)

    @pl.when(s == 0)
    def _():
        carry_ref[...] = jnp.zeros_like(carry_ref)

    g = proj(_GT, _NTOK) + gbias_ref[...]
    col = lax.broadcasted_iota(jnp.int32, (BLOCK, LANES), 1)
    is_i = (col >= _GI) & (col < _GF)
    is_fox = col < _GI
    def bf16_pieces(v):
        pieces = []
        for _ in range(FOX_BIAS_PIECES):
            part = v.astype(BF16)
            pieces.append(part)
            v = v - part.astype(F32)
        return pieces

    nchunk = tm // BLOCK
    val_pieces = bf16_pieces(_log_sigmoid(g))
    rhs = jnp.concatenate([piece[c * BLOCK:(c + 1) * BLOCK] for c in range(nchunk) for piece in val_pieces], axis=1)
    cs_all = jnp.dot(ltri_ref[...], rhs, preferred_element_type=F32)
    carry = carry_ref[...]
    outs = []
    for c in range(nchunk):
        blocks = [cs_all[:, (FOX_BIAS_PIECES * c + q) * LANES:(FOX_BIAS_PIECES * c + q + 1) * LANES]
                  for q in range(FOX_BIAS_PIECES)]
        cs = blocks[0] + blocks[1] + blocks[2] + jnp.where(is_fox, carry, 0.0)
        out_c = jnp.where(is_i, g[c * BLOCK:(c + 1) * BLOCK], cs)
        carry = out_c[BLOCK - 1:BLOCK, :]
        gtok_ref[0, c * BLOCK:(c + 1) * BLOCK, :] = out_c
        gT_ref[0, c] = out_c.T
        outs.append(out_c)
    carry_ref[...] = carry
    aug_pieces = bf16_pieces(jnp.concatenate(outs, axis=0) * (-LOG2E))
    aug = jnp.dot(jnp.concatenate(aug_pieces, axis=1), sel_ref[...], preferred_element_type=F32).astype(BF16)
    for pr in range(FOX_HEADS // 2):
        aug_ref[0, pr] = aug[:, pr * LANES:(pr + 1) * LANES]


def _in_proj(x, gain, wt, wT, fkg, sqg, skg, gm, cosf, sinf, gbias, ltri, sel, *, tm, tkv):
    B, S, D = x.shape
    nc = S // BLOCK
    const = lambda shape: pl.BlockSpec(shape, lambda b, s: (0,) * len(shape))
    tok = lambda w: pl.BlockSpec((1, tm, w), lambda b, s: (b, s, 0))
    out_shape = (
        jax.ShapeDtypeStruct((B, FOX_HEADS, LANES, S), BF16),
        jax.ShapeDtypeStruct((B, S // tkv, FOX_HEADS // 2, FOX_VROWS, tkv), BF16),
        jax.ShapeDtypeStruct((B, FOX_HEADS // 2, S, LANES), BF16),
        jax.ShapeDtypeStruct((B, S, 512), BF16),
        jax.ShapeDtypeStruct((B, S, 256), BF16),
        jax.ShapeDtypeStruct((B, S, 256), BF16),
        jax.ShapeDtypeStruct((B, S, 512), BF16),
        jax.ShapeDtypeStruct((B, S, 512), BF16),
        jax.ShapeDtypeStruct((B, S, 512), BF16),
        jax.ShapeDtypeStruct((B, S, 512), BF16),
        jax.ShapeDtypeStruct((B, S, LANES), F32),
        jax.ShapeDtypeStruct((B, nc, LANES, BLOCK), F32),
        jax.ShapeDtypeStruct((B, FOX_HEADS // 2, S, LANES), BF16),
    )
    out_specs = (
        pl.BlockSpec((1, FOX_HEADS, LANES, tm), lambda b, s: (b, 0, 0, s)),
        pl.BlockSpec((1, tm // tkv, FOX_HEADS // 2, FOX_VROWS, tkv), lambda b, s: (b, s, 0, 0, 0)),
        pl.BlockSpec((1, FOX_HEADS // 2, tm, LANES), lambda b, s: (b, 0, s, 0)),
        tok(512), tok(256), tok(256), tok(512), tok(512), tok(512), tok(512), tok(LANES),
        pl.BlockSpec((1, tm // BLOCK, LANES, BLOCK), lambda b, s: (b, s, 0, 0)),
        pl.BlockSpec((1, FOX_HEADS // 2, tm, LANES), lambda b, s: (b, 0, s, 0)),
    )
    in_specs = [
        pl.BlockSpec((1, tm, D), lambda b, s: (b, s, 0)),
        const((1, D)), const(wt.shape), const(wT.shape), const((1, 512)), const((1, 512)), const((1, 256)),
        const((512, 512)),
        pl.BlockSpec((tm, LANES), lambda b, s: (s, 0)), pl.BlockSpec((tm, LANES), lambda b, s: (s, 0)),
        const((1, LANES)), const((BLOCK, BLOCK)), const(sel.shape),
    ]
    return pl.pallas_call(
        functools.partial(_in_proj_kernel, tm=tm, tkv=tkv),
        grid=(B, S // tm), in_specs=in_specs, out_specs=out_specs, out_shape=out_shape,
        scratch_shapes=[pltpu.VMEM((1, LANES), F32)],
        compiler_params=pltpu.CompilerParams(dimension_semantics=("parallel", "arbitrary"),
                                             vmem_limit_bytes=VMEM_LIMIT_BYTES),
        name="in_proj",
    )(x, gain, wt, wT, fkg, sqg, skg, gm, cosf, sinf, gbias, ltri, sel)


def _col_reduce(x, op, ways=8):
    rows, cols = x.shape
    return op(op(x.reshape(ways, rows // ways, cols), axis=0), axis=0, keepdims=True)


def _fox_kernel(qT_ref, k_ref, vT_ref, aug_ref, o_ref, qq_ref, m_ref, l_ref, acc_ref, za_ref, p_ref, al_ref, *, tq):
    i = pl.program_id(1)
    NP = FOX_HEADS // 2
    G = FOX_GROUP
    NAUG = FOX_BIAS_PIECES

    row = lax.broadcasted_iota(jnp.int32, (LANES, 2 * tq), 0)
    col = lax.broadcasted_iota(jnp.int32, (LANES, 2 * tq), 1)
    ones_rows = jnp.where((row < 2 * NAUG) & ((row >= NAUG) == (col >= tq)), 1.0, 0.0).astype(BF16)
    for pr in range(NP):
        qq_ref[pr] = jnp.concatenate(
            [jnp.concatenate([qT_ref[0, 2 * pr], qT_ref[0, 2 * pr + 1]], axis=1), ones_rows], axis=0)
    m_ref[...] = jnp.full(m_ref.shape, NEG, F32)
    l_ref[...] = jnp.zeros(l_ref.shape, F32)
    acc_ref[...] = jnp.zeros(acc_ref.shape, F32)
    za_ref[...] = jnp.full(za_ref.shape, NEG, F32)
    p_ref[...] = jnp.zeros(p_ref.shape, BF16)
    al_ref[...] = jnp.ones(al_ref.shape, F32)
    krow = lax.broadcasted_iota(jnp.int32, (tq, 2 * tq), 0)
    qcol = lax.broadcasted_iota(jnp.int32, (tq, 2 * tq), 1)
    causal = krow <= (qcol & (tq - 1))
    n_end = NP * (i + 1)

    def where_is(n):
        tile = jnp.clip(n >> 2, 0, i)
        return n & (NP - 1), tile, pl.multiple_of(tile * tq, tq)

    def stage_scores(n, j, guard):
        pr, _, r0 = where_is(n)
        kk = jnp.concatenate([k_ref[0, pr, pl.ds(r0, tq), :], aug_ref[0, pr, pl.ds(r0, tq), :]], axis=1)
        s = jnp.dot(kk, qq_ref[pr], preferred_element_type=F32)
        za_ref[j] = jnp.where(n < n_end, s, NEG) if guard else s

    def stage_softmax(n, j, masked):
        pr, _, _ = where_is(n)
        z = za_ref[j]
        if masked:
            z = jnp.where(causal, z, NEG)
        m = m_ref[pr]
        m_new = jnp.maximum(m, _col_reduce(z, jnp.max))
        p_ref[j] = jnp.exp2(z - m_new).astype(BF16)
        al_ref[j] = jnp.exp2(m - m_new)
        m_ref[pr] = m_new

    def stage_values(n, j):
        pr, tile, _ = where_is(n)
        pv = jnp.dot(vT_ref[0, tile, pr], p_ref[j], preferred_element_type=F32)
        al = al_ref[j]
        acc_ref[2 * pr] = al[:, :tq] * acc_ref[2 * pr] + pv[:HEAD_DIM, :tq]
        acc_ref[2 * pr + 1] = al[:, tq:] * acc_ref[2 * pr + 1] + pv[HEAD_DIM:2 * HEAD_DIM, tq:]
        l_ref[pr] = al * l_ref[pr] + pv[2 * HEAD_DIM:2 * HEAD_DIM + 1, :]

    def step(g, masked, guard):
        for j in range(G):
            stage_values((g - 2) * G + j, j)
        for j in range(G):
            stage_softmax((g - 1) * G + j, j, masked)
        for j in range(G):
            stage_scores(g * G + j, j, guard)

    def body_plain(g, carry):
        step(g, False, False)
        return carry

    def body_diag(g, carry):
        step(g, True, True)
        return carry

    g_diag = (NP // G) * i + 1
    lax.fori_loop(0, g_diag, body_plain, 0)
    lax.fori_loop(g_diag, (NP // G) * (i + 1) + 2, body_diag, 0)
    for pr in range(NP):
        la, lb = l_ref[pr][:, :tq], l_ref[pr][:, tq:]
        oT = jnp.concatenate([acc_ref[2 * pr] / la, acc_ref[2 * pr + 1] / lb], axis=0)
        o_ref[0, :, pr * LANES:(pr + 1) * LANES] = oT.T.astype(BF16)


def _fox_attention(fqT, fk, fvT, faug, *, tq):
    B, NP, S, _ = fk.shape
    nkv = S // tq
    vrows = fvT.shape[3]
    return pl.pallas_call(
        functools.partial(_fox_kernel, tq=tq),
        grid=(B, S // tq),
        in_specs=[
            pl.BlockSpec((1, FOX_HEADS, LANES, tq), lambda b, i: (b, 0, 0, i)),
            pl.BlockSpec((1, NP, S, LANES), lambda b, i: (b, 0, 0, 0)),
            pl.BlockSpec((1, nkv, NP, vrows, tq), lambda b, i: (b, 0, 0, 0, 0)),
            pl.BlockSpec((1, NP, S, LANES), lambda b, i: (b, 0, 0, 0)),
        ],
        out_specs=pl.BlockSpec((1, tq, MIX_WIDTH), lambda b, i: (b, i, 0)),
        out_shape=jax.ShapeDtypeStruct((B, S, MIX_WIDTH), BF16),
        scratch_shapes=[
            pltpu.VMEM((NP, 2 * LANES, 2 * tq), BF16),
            pltpu.VMEM((NP, 1, 2 * tq), F32),
            pltpu.VMEM((NP, 1, 2 * tq), F32),
            pltpu.VMEM((FOX_HEADS, HEAD_DIM, tq), F32),
            pltpu.VMEM((FOX_GROUP, tq, 2 * tq), F32),
            pltpu.VMEM((FOX_GROUP, tq, 2 * tq), BF16),
            pltpu.VMEM((FOX_GROUP, 1, 2 * tq), F32),
        ],
        compiler_params=pltpu.CompilerParams(dimension_semantics=("parallel", "arbitrary"),
                                             vmem_limit_bytes=VMEM_LIMIT_BYTES),
        name="fox_attention",
    )(fqT, fk, fvT, faug)


def _swa_kernel(sinks_ref, q_ref, kp_ref, kc_ref, vp_ref, vc_ref, o_ref):
    n = pl.program_id(1)
    L = BLOCK
    row = lax.broadcasted_iota(jnp.int32, (L, 2 * L), 0)
    col = lax.broadcasted_iota(jnp.int32, (L, 2 * L), 1)
    row_prev = row + jnp.where(n == 0, 2 * L, 0)
    mask = ((col < L) & (col > row_prev)) | ((col >= L) & ((col - L) <= row))
    lane = lax.broadcasted_iota(jnp.int32, (L, LANES), 1)
    low = lane < HEAD_DIM
    q = q_ref[0]
    zero = jnp.zeros((L, LANES), BF16)
    for pair in range(SWA_Q_HEADS // 2):
        g = pair // 2
        kf = jnp.concatenate([kp_ref[0, :, g * LANES:(g + 1) * LANES], kc_ref[0, :, g * LANES:(g + 1) * LANES]], axis=0)
        vf = jnp.concatenate([vp_ref[0, :, g * LANES:(g + 1) * LANES], vc_ref[0, :, g * LANES:(g + 1) * LANES]], axis=0)
        qp = q[:, pair * LANES:(pair + 1) * LANES]
        outs = []
        for half in range(2):
            sink = sinks_ref[2 * pair + half]
            qh = jnp.where(low if half == 0 else ~low, qp, zero)
            z = jnp.where(mask, _nt_dot(qh, kf), NEG)
            m = jnp.maximum(jnp.max(z, axis=1, keepdims=True), sink)
            pm = jnp.exp(z - m)
            den = jnp.sum(pm, axis=1, keepdims=True) + jnp.exp(sink - m)
            outs.append(jnp.dot(pm.astype(BF16), vf, preferred_element_type=F32) / den)
        o_ref[0, :, pair * LANES:(pair + 1) * LANES] = jnp.where(low, outs[0], outs[1]).astype(BF16)


def _swa_attention(sinks, sq, sk, sv):
    B, S, _ = sq.shape
    prev = lambda b, n: (b, jnp.maximum(n - 1, 0), 0)
    cur = lambda b, n: (b, n, 0)
    return pl.pallas_call(
        _swa_kernel,
        grid=(B, S // BLOCK),
        in_specs=[
            pl.BlockSpec(memory_space=pltpu.SMEM),
            pl.BlockSpec((1, BLOCK, 512), cur),
            pl.BlockSpec((1, BLOCK, 256), prev), pl.BlockSpec((1, BLOCK, 256), cur),
            pl.BlockSpec((1, BLOCK, 256), prev), pl.BlockSpec((1, BLOCK, 256), cur),
        ],
        out_specs=pl.BlockSpec((1, BLOCK, 512), cur),
        out_shape=jax.ShapeDtypeStruct((B, S, MIX_WIDTH), BF16),
        compiler_params=pltpu.CompilerParams(dimension_semantics=("parallel", "parallel"),
                                             vmem_limit_bytes=VMEM_LIMIT_BYTES),
        name="swa_attention",
    )(sinks, sq, sk, sk, sv, sv)


def _mlstm_kernel(qp_ref, q_ref, kp_ref, k_ref, v_ref, og_ref, gt_ref, gT_ref, sh_ref, cw_ref, cb_ref, on_ref,
                  y_ref, c_ref, n_ref, m_ref):
    c = pl.program_id(1)
    L = BLOCK

    @pl.when(c == 0)
    def _():
        c_ref[...] = jnp.zeros_like(c_ref)
        n_ref[...] = jnp.zeros_like(n_ref)
        m_ref[...] = jnp.zeros_like(m_ref)

    def conv_silu(p_ref, x_ref, off):
        cur = x_ref[0]
        prev = jnp.where(c > 0, p_ref[0], jnp.zeros_like(cur))
        sh = jnp.dot(sh_ref[...], jnp.concatenate([prev, cur], axis=0), preferred_element_type=F32)
        y = cb_ref[:, off:off + MIX_WIDTH]
        for j in range(CONV_WIDTH - 1):
            y = y + sh[j * L:(j + 1) * L] * cw_ref[j:j + 1, off:off + MIX_WIDTH]
        y = y + cur.astype(F32) * cw_ref[CONV_WIDTH - 1:CONV_WIDTH, off:off + MIX_WIDTH]
        return y * _sigmoid(y)

    qc = conv_silu(qp_ref, q_ref, 0)
    kc = conv_silu(kp_ref, k_ref, MIX_WIDTH) * (MLSTM_HEAD_DIM ** -0.5)
    gt = gt_ref[0]
    gT = gT_ref[0, 0]
    tri = lax.broadcasted_iota(jnp.int32, (L, L), 0) >= lax.broadcasted_iota(jnp.int32, (L, L), 1)

    for hd in range(MLSTM_HEADS):
        sl = slice(hd * MLSTM_HEAD_DIM, (hd + 1) * MLSTM_HEAD_DIM)
        q = qc[:, sl]
        k = kc[:, sl]
        qb, kb = q.astype(BF16), k.astype(BF16)
        v = v_ref[0, :, sl]
        i_col, b_col = gt[:, _GI + hd:_GI + hd + 1], gt[:, _GF + hd:_GF + hd + 1]
        i_row, b_row = gT[_GI + hd:_GI + hd + 1, :], gT[_GF + hd:_GF + hd + 1, :]
        b_last = b_col[L - 1:L, :]

        dmat = jnp.where(tri, b_col - b_row + i_row, NEG)
        m_loc = jnp.max(dmat, axis=1, keepdims=True)
        smat = jnp.exp(dmat - m_loc) * _nt_dot(qb, kb)
        s_v = jnp.dot(smat.astype(BF16), v, preferred_element_type=F32)
        s_sum = jnp.sum(smat, axis=1, keepdims=True)
        g_col = b_last - b_col + i_col
        g_max = jnp.max(b_last - b_row + i_row, axis=1, keepdims=True)
        kw = jnp.exp(g_col - g_max) * k
        kv_new = _tn_dot(kw.astype(BF16), v)
        k_new = jnp.sum(kw, axis=0, keepdims=True)

        m_prev = m_ref[hd]
        cmat = c_ref[hd]
        n_row = n_ref[hd]
        a_col = b_col + m_prev
        mt = jnp.maximum(a_col, m_loc)
        w_inter = jnp.exp(a_col - mt)
        w_intra = jnp.exp(m_loc - mt)
        num = w_inter * jnp.dot(qb, cmat.astype(BF16), preferred_element_type=F32) + w_intra * s_v
        den = w_inter * jnp.sum(q * n_row, axis=1, keepdims=True) + w_intra * s_sum
        hcur = num / jnp.maximum(jnp.abs(den), jnp.exp(-mt))

        m_new = jnp.maximum(b_last + m_prev, g_max)
        decay = jnp.exp(b_last + m_prev - m_new)
        w_new = jnp.exp(g_max - m_new)
        c_ref[hd] = decay * cmat + w_new * kv_new
        n_ref[hd] = decay * n_row + w_new * k_new
        m_ref[hd] = m_new

        hn = _rmsnorm_rows(hcur, on_ref[:, sl])
        y_ref[0, :, sl] = (hn * og_ref[0, :, sl].astype(F32)).astype(BF16)


def _mlstm(mq, mk, mv, og, gtok, gT, shifts, conv_w, conv_b, out_norm):
    B, S, _ = mq.shape
    cur = lambda b, c: (b, c, 0)
    prev = lambda b, c: (b, jnp.maximum(c - 1, 0), 0)
    tok = lambda w: pl.BlockSpec((1, BLOCK, w), cur)
    const = lambda shape: pl.BlockSpec(shape, lambda b, c: (0,) * len(shape))
    return pl.pallas_call(
        _mlstm_kernel,
        grid=(B, S // BLOCK),
        in_specs=[pl.BlockSpec((1, BLOCK, 512), prev), tok(512), pl.BlockSpec((1, BLOCK, 512), prev), tok(512),
                  tok(512), tok(512), tok(LANES),
                  pl.BlockSpec((1, 1, LANES, BLOCK), lambda b, c: (b, c, 0, 0)),
                  const(shifts.shape), const((CONV_WIDTH, 2 * MIX_WIDTH)), const((1, 2 * MIX_WIDTH)),
                  const((1, MIX_WIDTH))],
        out_specs=tok(512),
        out_shape=jax.ShapeDtypeStruct((B, S, MIX_WIDTH), BF16),
        scratch_shapes=[
            pltpu.VMEM((MLSTM_HEADS, MLSTM_HEAD_DIM, MLSTM_HEAD_DIM), F32),
            pltpu.VMEM((MLSTM_HEADS, 1, MLSTM_HEAD_DIM), F32),
            pltpu.VMEM((MLSTM_HEADS, 1, 1), F32),
        ],
        compiler_params=pltpu.CompilerParams(dimension_semantics=("parallel", "arbitrary"),
                                             vmem_limit_bytes=VMEM_LIMIT_BYTES),
        name="mlstm",
    )(mq, mq, mk, mk, mv, og, gtok, gT, shifts, conv_w, conv_b, out_norm)


def _merge_kernel(x_ref, g_ref, yf_ref, ys_ref, ym_ref, wgl_ref, wb_ref, wo_ref, o_ref):
    x = x_ref[...]
    h = _rmsnorm_rows(x, g_ref[...]).astype(BF16)
    merged = None
    for br, y_ref in enumerate((yf_ref, ys_ref, ym_ref)):
        gate = _sigmoid(jnp.dot(h, wgl_ref[:, br * D_MODEL:(br + 1) * D_MODEL], preferred_element_type=F32))
        term = gate * jnp.dot(y_ref[...], wb_ref[br], preferred_element_type=F32)
        merged = term if merged is None else merged + term
    o_ref[...] = x + jnp.dot(merged.astype(BF16), wo_ref[...], preferred_element_type=F32)


def _merge(x2, gain, yf, ys, ym, wgl, wb, wo, *, tm):
    T, D = x2.shape
    row = lambda w: pl.BlockSpec((tm, w), lambda t: (t, 0))
    const = lambda shape: pl.BlockSpec(shape, lambda t: (0,) * len(shape))
    return pl.pallas_call(
        _merge_kernel,
        grid=(T // tm,),
        in_specs=[row(D), const((1, D)), row(512), row(512), row(512),
                  const(wgl.shape), const(wb.shape), const(wo.shape)],
        out_specs=row(D),
        out_shape=jax.ShapeDtypeStruct((T, D), F32),
        compiler_params=pltpu.CompilerParams(dimension_semantics=("parallel",),
                                             vmem_limit_bytes=VMEM_LIMIT_BYTES),
        name="merge_out",
    )(x2, gain, yf, ys, ym, wgl, wb, wo)


def _mlp_kernel(x_ref, g_ref, wu_ref, wd_ref, o_ref, *, fc):
    x = x_ref[...]
    h = _rmsnorm_rows(x, g_ref[...]).astype(BF16)
    acc = x
    for f in range(D_FF // fc):
        a = jnp.maximum(jnp.dot(h, wu_ref[:, f * fc:(f + 1) * fc], preferred_element_type=F32), 0.0)
        acc = acc + jnp.dot((a * a).astype(BF16), wd_ref[f * fc:(f + 1) * fc, :], preferred_element_type=F32)
    o_ref[...] = acc


def _mlp(x2, gain, wu, wd, *, tm, fc):
    T, D = x2.shape
    row = pl.BlockSpec((tm, D), lambda t: (t, 0))
    const = lambda shape: pl.BlockSpec(shape, lambda t: (0,) * len(shape))
    return pl.pallas_call(
        functools.partial(_mlp_kernel, fc=fc),
        grid=(T // tm,),
        in_specs=[row, const((1, D)), const(wu.shape), const(wd.shape)],
        out_specs=row,
        out_shape=jax.ShapeDtypeStruct((T, D), F32),
        compiler_params=pltpu.CompilerParams(dimension_semantics=("parallel",),
                                             vmem_limit_bytes=VMEM_LIMIT_BYTES),
        name="mlp",
    )(x2, gain, wu, wd)


def _tiles(S):
    tm = 512 if S % 512 == 0 else S
    tq = 256 if S % 256 == 0 else S
    return tm, tq


def _layer_params(w_in, fox_f_bias, fox_q_norm, fox_k_norm, swa_q_norm, swa_k_norm, mlstm_i_bias, mlstm_f_bias):
    offs = np.cumsum(IN_SPLITS)[:-1].tolist()
    fq, fk, fv, ff, sq, sk, sv, mq, mk, mv, mi, mf, mo, gl = jnp.split(w_in, offs, axis=1)
    dup = lambda w: jnp.concatenate([w[:, :HEAD_DIM], w[:, :HEAD_DIM], w[:, HEAD_DIM:], w[:, HEAD_DIM:]], axis=1)
    gates = jnp.zeros((D_MODEL, LANES), F32)
    gbias = jnp.zeros((1, LANES), F32)
    for hh in range(FOX_HEADS):
        colx = 8 * (hh // 2) + hh % 2
        gates = gates.at[:, colx].set(ff[:, hh])
        gbias = gbias.at[0, colx].set(fox_f_bias[hh])
    gates = gates.at[:, _GI:_GI + MLSTM_HEADS].set(mi).at[:, _GF:_GF + MLSTM_HEADS].set(mf)
    gbias = gbias.at[0, _GI:_GI + MLSTM_HEADS].set(mlstm_i_bias).at[0, _GF:_GF + MLSTM_HEADS].set(mlstm_f_bias)
    wt = jnp.concatenate([fk, sq, dup(sk), dup(sv), mq, mk, mv, mo, gates], axis=1).astype(BF16)
    wT = jnp.concatenate([fq, fv], axis=1).T.astype(BF16)
    fkg = jnp.tile(fox_k_norm * fox_q_norm * (HEAD_DIM ** -0.5 * LOG2E), FOX_HEADS)[None, :]
    sqg = jnp.tile(swa_q_norm * (HEAD_DIM ** -0.5), SWA_Q_HEADS)[None, :]
    skg = jnp.tile(swa_k_norm, 2 * SWA_KV_HEADS)[None, :]
    return wt, wT, fkg, sqg, skg, gbias, gl.astype(BF16)


def kernel(x, norm_mix, w_in, fox_f_bias, fox_q_norm, fox_k_norm, swa_q_norm, swa_k_norm, swa_sinks, conv_w,
           conv_b, mlstm_i_bias, mlstm_f_bias, mlstm_out_norm, w_branch, w_out, norm_mlp, w_up, w_down):
    B, S, D = x.shape
    depth = w_in.shape[0]
    tm, tq = _tiles(S)

    inv = ROPE_THETA ** (-jnp.arange(0, HEAD_DIM, 2, dtype=F32) / HEAD_DIM)
    ang = jnp.arange(S, dtype=F32)[:, None] * inv[None, :]
    cos, sin = jnp.cos(ang), jnp.sin(ang)
    cosf = jnp.tile(jnp.concatenate([cos, cos], axis=1), (1, LANES // HEAD_DIM))
    sinf = jnp.tile(jnp.concatenate([-sin, sin], axis=1), (1, LANES // HEAD_DIM))
    gm = jnp.asarray(np.kron(np.eye(MIX_WIDTH // HEAD_DIM), np.full((HEAD_DIM, HEAD_DIM), 1.0 / HEAD_DIM)), BF16)
    ltri = jnp.asarray(np.tril(np.ones((BLOCK, BLOCK))), BF16)
    shifts_np = np.zeros(((CONV_WIDTH - 1) * BLOCK, 2 * BLOCK))
    for j in range(CONV_WIDTH - 1):
        for t in range(BLOCK):
            shifts_np[j * BLOCK + t, BLOCK + t - (CONV_WIDTH - 1 - j)] = 1.0
    shifts = jnp.asarray(shifts_np, BF16)
    sel_np = np.zeros((FOX_BIAS_PIECES * LANES, MIX_WIDTH))
    for piece in range(FOX_BIAS_PIECES):
        for hd in range(FOX_HEADS):
            sel_np[piece * LANES + 8 * (hd // 2) + hd % 2, LANES * (hd // 2) + FOX_BIAS_PIECES * (hd % 2) + piece] = 1.0
    sel = jnp.asarray(sel_np, BF16)

    for l in range(depth):
        wt, wT, fkg, sqg, skg, gbias, wgl = _layer_params(
            w_in[l], fox_f_bias[l], fox_q_norm[l], fox_k_norm[l], swa_q_norm[l], swa_k_norm[l],
            mlstm_i_bias[l], mlstm_f_bias[l])
        (fqT, fvT, fk, sq, sk, sv, mq, mk, mv, og, gtok, gT, faug) = _in_proj(
            x, norm_mix[l][None, :], wt, wT, fkg, sqg, skg, gm, cosf, sinf, gbias, ltri, sel, tm=tm, tkv=tq)
        y_fox = _fox_attention(fqT, fk, fvT, faug, tq=tq)
        y_swa = _swa_attention(swa_sinks[l], sq, sk, sv)
        y_ml = _mlstm(mq, mk, mv, og, gtok, gT, shifts, conv_w[l], conv_b[l][None, :], mlstm_out_norm[l][None, :])
        x2 = _merge(x.reshape(B * S, D), norm_mix[l][None, :], y_fox.reshape(B * S, -1), y_swa.reshape(B * S, -1),
                    y_ml.reshape(B * S, -1), wgl, w_branch[l].astype(BF16), w_out[l].astype(BF16), tm=tm)
        x2 = _mlp(x2, norm_mlp[l][None, :], w_up[l].astype(BF16), w_down[l].astype(BF16), tm=tm, fc=1024)
        x = x2.reshape(B, S, D)
    return x
```

```python
import functools

import jax
import jax.numpy as jnp
import numpy as np
from jax import lax
from jax.experimental import pallas as pl
from jax.experimental.pallas import tpu as pltpu

F32 = jnp.float32
BF16 = jnp.bfloat16

D_MODEL = 1024
HEAD_DIM = 64
MIX_WIDTH = 512
FOX_HEADS = 8
SWA_Q_HEADS = 8
SWA_KV_HEADS = 2
SWA_WINDOW = 128
MLSTM_HEADS = 4
MLSTM_HEAD_DIM = 128
CONV_WIDTH = 4
D_FF = 4 * D_MODEL
N_BRANCHES = 3
BLOCK = 128
ROPE_THETA = 10000.0
EPS = 1e-6
NEG = -1e30
LOG2E = 1.4426950408889634
FOX_GROUP = 2

LANES = 128
FOX_VROWS = LANES + 16
FOX_BIAS_PIECES = 3
VMEM_LIMIT_BYTES = 56 * 1024 * 1024

_FK, _SQ, _SK, _SV, _MQ, _MK, _MV, _MO, _GT, _NTOK = 0, 512, 1024, 1280, 1536, 2048, 2560, 3072, 3584, 3712
_GI, _GF = 32, 36

IN_SPLITS = (512, 512, 512, 8, 512, 128, 128, 512, 512, 512, 4, 4, 512, 3 * D_MODEL)


def _nt_dot(a, b):
    return lax.dot_general(a, b, (((1,), (1,)), ((), ())), preferred_element_type=F32)


def _tn_dot(a, b):
    return lax.dot_general(a, b, (((0,), (0,)), ((), ())), preferred_element_type=F32)


def _sigmoid(x):
    return 1.0 / (1.0 + jnp.exp(-x))


def _log_sigmoid(x):
    return jnp.minimum(x, 0.0) - jnp.log(1.0 + jnp.exp(-jnp.abs(x)))


def _rmsnorm_rows(x, gain_row):
    ms = jnp.mean(x * x, axis=-1, keepdims=True)
    return x * lax.rsqrt(ms + EPS) * gain_row


def _in_proj_kernel(x_ref, g_ref, wt_ref, wT_ref, fkg_ref, sqg_ref, skg_ref, gm_ref, cos_ref, sin_ref,
                    gbias_ref, ltri_ref, sel_ref,
                    fqT_ref, fvT_ref, fk_ref, sq_ref, sk_ref, sv_ref, mq_ref, mk_ref, mv_ref, og_ref,
                    gtok_ref, gT_ref, aug_ref, carry_ref, *, tm, tkv):
    s = pl.program_id(1)
    h = _rmsnorm_rows(x_ref[0], g_ref[...]).astype(BF16)

    def proj(lo, hi):
        return jnp.dot(h, wt_ref[:, lo:hi], preferred_element_type=F32)

    def head_ms(u):
        w = u.shape[1]
        return jnp.dot((u * u).astype(BF16), gm_ref[:w, :w], preferred_element_type=F32)

    def rope(u):
        lane = lax.broadcasted_iota(jnp.int32, (tm, LANES), 1)
        first = (lane % HEAD_DIM) < (HEAD_DIM // 2)
        c, sn = cos_ref[...], sin_ref[...]
        outs = []
        for j in range(u.shape[1] // LANES):
            uj = u[:, j * LANES:(j + 1) * LANES]
            partner = jnp.where(first, pltpu.roll(uj, LANES - HEAD_DIM // 2, 1), pltpu.roll(uj, HEAD_DIM // 2, 1))
            outs.append(uj * c + partner * sn)
        return jnp.concatenate(outs, axis=1)

    qT = _nt_dot(wT_ref[0:MIX_WIDTH, :], h)
    msT = jnp.dot(gm_ref[...], (qT * qT).astype(BF16), preferred_element_type=F32)
    qn = (qT * lax.rsqrt(msT + EPS)).astype(BF16)
    pad = jnp.zeros((HEAD_DIM, tm), BF16)
    for hd in range(FOX_HEADS):
        rows = qn[hd * HEAD_DIM:(hd + 1) * HEAD_DIM]
        fqT_ref[0, hd] = jnp.concatenate([rows, pad] if hd % 2 == 0 else [pad, rows], axis=0)
    vT = _nt_dot(wT_ref[MIX_WIDTH:2 * MIX_WIDTH, :], h).astype(BF16)
    ones = jnp.ones((FOX_VROWS - LANES, tkv), BF16)
    for c in range(tm // tkv):
        for pr in range(FOX_HEADS // 2):
            fvT_ref[0, c, pr] = jnp.concatenate([vT[pr * LANES:(pr + 1) * LANES, c * tkv:(c + 1) * tkv], ones], axis=0)

    u = proj(_FK, _SQ)
    kn = (u * lax.rsqrt(head_ms(u) + EPS) * fkg_ref[...]).astype(BF16)
    for pr in range(FOX_HEADS // 2):
        fk_ref[0, pr] = kn[:, pr * LANES:(pr + 1) * LANES]

    u = proj(_SQ, _SK)
    sq_ref[0] = rope(u * lax.rsqrt(head_ms(u) + EPS) * sqg_ref[...]).astype(BF16)
    u = proj(_SK, _SV)
    sk_ref[0] = rope(u * lax.rsqrt(head_ms(u) + EPS) * skg_ref[...]).astype(BF16)
    sv_ref[0] = proj(_SV, _MQ).astype(BF16)

    mq_ref[0] = proj(_MQ, _MK).astype(BF16)
    mk_ref[0] = proj(_MK, _MV).astype(BF16)
    mv_ref[0] = proj(_MV, _MO).astype(BF16)
    og_ref[0] = _sigmoid(proj(_MO, _GT)).astype(BF16)

    @pl.when(s == 0)
    def _():
        carry_ref[...] = jnp.zeros_like(carry_ref)

    g = proj(_GT, _NTOK) + gbias_ref[...]
    col = lax.broadcasted_iota(jnp.int32, (BLOCK, LANES), 1)
    is_i = (col >= _GI) & (col < _GF)
    is_fox = col < _GI
    def bf16_pieces(v):
        pieces = []
        for _ in range(FOX_BIAS_PIECES):
            part = v.astype(BF16)
            pieces.append(part)
            v = v - part.astype(F32)
        return pieces

    nchunk = tm // BLOCK
    val_pieces = bf16_pieces(_log_sigmoid(g))
    rhs = jnp.concatenate([piece[c * BLOCK:(c + 1) * BLOCK] for c in range(nchunk) for piece in val_pieces], axis=1)
    cs_all = jnp.dot(ltri_ref[...], rhs, preferred_element_type=F32)
    carry = carry_ref[...]
    outs = []
    for c in range(nchunk):
        blocks = [cs_all[:, (FOX_BIAS_PIECES * c + q) * LANES:(FOX_BIAS_PIECES * c + q + 1) * LANES]
                  for q in range(FOX_BIAS_PIECES)]
        cs = blocks[0] + blocks[1] + blocks[2] + jnp.where(is_fox, carry, 0.0)
        out_c = jnp.where(is_i, g[c * BLOCK:(c + 1) * BLOCK], cs)
        carry = out_c[BLOCK - 1:BLOCK, :]
        gtok_ref[0, c * BLOCK:(c + 1) * BLOCK, :] = out_c
        gT_ref[0, c] = out_c.T
        outs.append(out_c)
    carry_ref[...] = carry
    aug_pieces = bf16_pieces(jnp.concatenate(outs, axis=0) * (-LOG2E))
    aug = jnp.dot(jnp.concatenate(aug_pieces, axis=1), sel_ref[...], preferred_element_type=F32).astype(BF16)
    for pr in range(FOX_HEADS // 2):
        aug_ref[0, pr] = aug[:, pr * LANES:(pr + 1) * LANES]


def _in_proj(x, gain, wt, wT, fkg, sqg, skg, gm, cosf, sinf, gbias, ltri, sel, *, tm, tkv):
    B, S, D = x.shape
    nc = S // BLOCK
    const = lambda shape: pl.BlockSpec(shape, lambda b, s: (0,) * len(shape))
    tok = lambda w: pl.BlockSpec((1, tm, w), lambda b, s: (b, s, 0))
    out_shape = (
        jax.ShapeDtypeStruct((B, FOX_HEADS, LANES, S), BF16),
        jax.ShapeDtypeStruct((B, S // tkv, FOX_HEADS // 2, FOX_VROWS, tkv), BF16),
        jax.ShapeDtypeStruct((B, FOX_HEADS // 2, S, LANES), BF16),
        jax.ShapeDtypeStruct((B, S, 512), BF16),
        jax.ShapeDtypeStruct((B, S, 256), BF16),
        jax.ShapeDtypeStruct((B, S, 256), BF16),
        jax.ShapeDtypeStruct((B, S, 512), BF16),
        jax.ShapeDtypeStruct((B, S, 512), BF16),
        jax.ShapeDtypeStruct((B, S, 512), BF16),
        jax.ShapeDtypeStruct((B, S, 512), BF16),
        jax.ShapeDtypeStruct((B, S, LANES), F32),
        jax.ShapeDtypeStruct((B, nc, LANES, BLOCK), F32),
        jax.ShapeDtypeStruct((B, FOX_HEADS // 2, S, LANES), BF16),
    )
    out_specs = (
        pl.BlockSpec((1, FOX_HEADS, LANES, tm), lambda b, s: (b, 0, 0, s)),
        pl.BlockSpec((1, tm // tkv, FOX_HEADS // 2, FOX_VROWS, tkv), lambda b, s: (b, s, 0, 0, 0)),
        pl.BlockSpec((1, FOX_HEADS // 2, tm, LANES), lambda b, s: (b, 0, s, 0)),
        tok(512), tok(256), tok(256), tok(512), tok(512), tok(512), tok(512), tok(LANES),
        pl.BlockSpec((1, tm // BLOCK, LANES, BLOCK), lambda b, s: (b, s, 0, 0)),
        pl.BlockSpec((1, FOX_HEADS // 2, tm, LANES), lambda b, s: (b, 0, s, 0)),
    )
    in_specs = [
        pl.BlockSpec((1, tm, D), lambda b, s: (b, s, 0)),
        const((1, D)), const(wt.shape), const(wT.shape), const((1, 512)), const((1, 512)), const((1, 256)),
        const((512, 512)),
        pl.BlockSpec((tm, LANES), lambda b, s: (s, 0)), pl.BlockSpec((tm, LANES), lambda b, s: (s, 0)),
        const((1, LANES)), const((BLOCK, BLOCK)), const(sel.shape),
    ]
    return pl.pallas_call(
        functools.partial(_in_proj_kernel, tm=tm, tkv=tkv),
        grid=(B, S // tm), in_specs=in_specs, out_specs=out_specs, out_shape=out_shape,
        scratch_shapes=[pltpu.VMEM((1, LANES), F32)],
        compiler_params=pltpu.CompilerParams(dimension_semantics=("parallel", "arbitrary"),
                                             vmem_limit_bytes=VMEM_LIMIT_BYTES),
        name="in_proj",
    )(x, gain, wt, wT, fkg, sqg, skg, gm, cosf, sinf, gbias, ltri, sel)


def _col_reduce(x, op, ways=8):
    rows, cols = x.shape
    return op(op(x.reshape(ways, rows // ways, cols), axis=0), axis=0, keepdims=True)


def _fox_kernel(qT_ref, k_ref, vT_ref, aug_ref, o_ref, qq_ref, m_ref, l_ref, acc_ref, za_ref, p_ref, al_ref, *, tq):
    i = pl.program_id(1)
    NP = FOX_HEADS // 2
    G = FOX_GROUP
    NAUG = FOX_BIAS_PIECES

    row = lax.broadcasted_iota(jnp.int32, (LANES, 2 * tq), 0)
    col = lax.broadcasted_iota(jnp.int32, (LANES, 2 * tq), 1)
    ones_rows = jnp.where((row < 2 * NAUG) & ((row >= NAUG) == (col >= tq)), 1.0, 0.0).astype(BF16)
    for pr in range(NP):
        qq_ref[pr] = jnp.concatenate(
            [jnp.concatenate([qT_ref[0, 2 * pr], qT_ref[0, 2 * pr + 1]], axis=1), ones_rows], axis=0)
    m_ref[...] = jnp.full(m_ref.shape, NEG, F32)
    l_ref[...] = jnp.zeros(l_ref.shape, F32)
    acc_ref[...] = jnp.zeros(acc_ref.shape, F32)
    za_ref[...] = jnp.full(za_ref.shape, NEG, F32)
    p_ref[...] = jnp.zeros(p_ref.shape, BF16)
    al_ref[...] = jnp.ones(al_ref.shape, F32)
    krow = lax.broadcasted_iota(jnp.int32, (tq, 2 * tq), 0)
    qcol = lax.broadcasted_iota(jnp.int32, (tq, 2 * tq), 1)
    causal = krow <= (qcol & (tq - 1))
    n_end = NP * (i + 1)

    def where_is(n):
        tile = jnp.clip(n >> 2, 0, i)
        return n & (NP - 1), tile, pl.multiple_of(tile * tq, tq)

    def stage_scores(n, j, guard):
        pr, _, r0 = where_is(n)
        kk = jnp.concatenate([k_ref[0, pr, pl.ds(r0, tq), :], aug_ref[0, pr, pl.ds(r0, tq), :]], axis=1)
        s = jnp.dot(kk, qq_ref[pr], preferred_element_type=F32)
        za_ref[j] = jnp.where(n < n_end, s, NEG) if guard else s

    def stage_softmax(n, j, masked):
        pr, _, _ = where_is(n)
        z = za_ref[j]
        if masked:
            z = jnp.where(causal, z, NEG)
        m = m_ref[pr]
        m_new = jnp.maximum(m, _col_reduce(z, jnp.max))
        p_ref[j] = jnp.exp2(z - m_new).astype(BF16)
        al_ref[j] = jnp.exp2(m - m_new)
        m_ref[pr] = m_new

    def stage_values(n, j):
        pr, tile, _ = where_is(n)
        pv = jnp.dot(vT_ref[0, tile, pr], p_ref[j], preferred_element_type=F32)
        al = al_ref[j]
        acc_ref[2 * pr] = al[:, :tq] * acc_ref[2 * pr] + pv[:HEAD_DIM, :tq]
        acc_ref[2 * pr + 1] = al[:, tq:] * acc_ref[2 * pr + 1] + pv[HEAD_DIM:2 * HEAD_DIM, tq:]
        l_ref[pr] = al * l_ref[pr] + pv[2 * HEAD_DIM:2 * HEAD_DIM + 1, :]

    def step(g, masked, guard):
        for j in range(G):
            stage_values((g - 2) * G + j, j)
        for j in range(G):
            stage_softmax((g - 1) * G + j, j, masked)
        for j in range(G):
            stage_scores(g * G + j, j, guard)

    def body_plain(g, carry):
        step(g, False, False)
        return carry

    def body_diag(g, carry):
        step(g, True, False)
        return carry

    g_diag = (NP // G) * i + 1
    g_end = (NP // G) * (i + 1)
    lax.fori_loop(0, g_diag, body_plain, 0)
    lax.fori_loop(g_diag, g_end, body_diag, 0)
    for j in range(G):
        stage_values((g_end - 2) * G + j, j)
    for j in range(G):
        stage_softmax((g_end - 1) * G + j, j, True)
    for j in range(G):
        stage_values((g_end - 1) * G + j, j)
    for pr in range(NP):
        la, lb = l_ref[pr][:, :tq], l_ref[pr][:, tq:]
        oT = jnp.concatenate([acc_ref[2 * pr] / la, acc_ref[2 * pr + 1] / lb], axis=0)
        o_ref[0, :, pr * LANES:(pr + 1) * LANES] = oT.T.astype(BF16)


def _fox_attention(fqT, fk, fvT, faug, *, tq):
    B, NP, S, _ = fk.shape
    nkv = S // tq
    vrows = fvT.shape[3]
    return pl.pallas_call(
        functools.partial(_fox_kernel, tq=tq),
        grid=(B, S // tq),
        in_specs=[
            pl.BlockSpec((1, FOX_HEADS, LANES, tq), lambda b, i: (b, 0, 0, i)),
            pl.BlockSpec((1, NP, S, LANES), lambda b, i: (b, 0, 0, 0)),
            pl.BlockSpec((1, nkv, NP, vrows, tq), lambda b, i: (b, 0, 0, 0, 0)),
            pl.BlockSpec((1, NP, S, LANES), lambda b, i: (b, 0, 0, 0)),
        ],
        out_specs=pl.BlockSpec((1, tq, MIX_WIDTH), lambda b, i: (b, i, 0)),
        out_shape=jax.ShapeDtypeStruct((B, S, MIX_WIDTH), BF16),
        scratch_shapes=[
            pltpu.VMEM((NP, 2 * LANES, 2 * tq), BF16),
            pltpu.VMEM((NP, 1, 2 * tq), F32),
            pltpu.VMEM((NP, 1, 2 * tq), F32),
            pltpu.VMEM((FOX_HEADS, HEAD_DIM, tq), F32),
            pltpu.VMEM((FOX_GROUP, tq, 2 * tq), F32),
            pltpu.VMEM((FOX_GROUP, tq, 2 * tq), BF16),
            pltpu.VMEM((FOX_GROUP, 1, 2 * tq), F32),
        ],
        compiler_params=pltpu.CompilerParams(dimension_semantics=("parallel", "arbitrary"),
                                             vmem_limit_bytes=VMEM_LIMIT_BYTES),
        name="fox_attention",
    )(fqT, fk, fvT, faug)


def _swa_kernel(sinks_ref, q_ref, kp_ref, kc_ref, vp_ref, vc_ref, o_ref):
    n = pl.program_id(1)
    L = BLOCK
    row = lax.broadcasted_iota(jnp.int32, (L, 2 * L), 0)
    col = lax.broadcasted_iota(jnp.int32, (L, 2 * L), 1)
    row_prev = row + jnp.where(n == 0, 2 * L, 0)
    mask = ((col < L) & (col > row_prev)) | ((col >= L) & ((col - L) <= row))
    lane = lax.broadcasted_iota(jnp.int32, (L, LANES), 1)
    low = lane < HEAD_DIM
    q = q_ref[0]
    zero = jnp.zeros((L, LANES), BF16)
    for pair in range(SWA_Q_HEADS // 2):
        g = pair // 2
        kf = jnp.concatenate([kp_ref[0, :, g * LANES:(g + 1) * LANES], kc_ref[0, :, g * LANES:(g + 1) * LANES]], axis=0)
        vf = jnp.concatenate([vp_ref[0, :, g * LANES:(g + 1) * LANES], vc_ref[0, :, g * LANES:(g + 1) * LANES]], axis=0)
        qp = q[:, pair * LANES:(pair + 1) * LANES]
        outs = []
        for half in range(2):
            sink = sinks_ref[2 * pair + half]
            qh = jnp.where(low if half == 0 else ~low, qp, zero)
            z = jnp.where(mask, _nt_dot(qh, kf), NEG)
            m = jnp.maximum(jnp.max(z, axis=1, keepdims=True), sink)
            pm = jnp.exp(z - m)
            den = jnp.sum(pm, axis=1, keepdims=True) + jnp.exp(sink - m)
            outs.append(jnp.dot(pm.astype(BF16), vf, preferred_element_type=F32) / den)
        o_ref[0, :, pair * LANES:(pair + 1) * LANES] = jnp.where(low, outs[0], outs[1]).astype(BF16)


def _swa_attention(sinks, sq, sk, sv):
    B, S, _ = sq.shape
    prev = lambda b, n: (b, jnp.maximum(n - 1, 0), 0)
    cur = lambda b, n: (b, n, 0)
    return pl.pallas_call(
        _swa_kernel,
        grid=(B, S // BLOCK),
        in_specs=[
            pl.BlockSpec(memory_space=pltpu.SMEM),
            pl.BlockSpec((1, BLOCK, 512), cur),
            pl.BlockSpec((1, BLOCK, 256), prev), pl.BlockSpec((1, BLOCK, 256), cur),
            pl.BlockSpec((1, BLOCK, 256), prev), pl.BlockSpec((1, BLOCK, 256), cur),
        ],
        out_specs=pl.BlockSpec((1, BLOCK, 512), cur),
        out_shape=jax.ShapeDtypeStruct((B, S, MIX_WIDTH), BF16),
        compiler_params=pltpu.CompilerParams(dimension_semantics=("parallel", "parallel"),
                                             vmem_limit_bytes=VMEM_LIMIT_BYTES),
        name="swa_attention",
    )(sinks, sq, sk, sk, sv, sv)


def _mlstm_kernel(qp_ref, q_ref, kp_ref, k_ref, v_ref, og_ref, gt_ref, gT_ref, sh_ref, cw_ref, cb_ref, on_ref,
                  y_ref, c_ref, n_ref, m_ref):
    c = pl.program_id(1)
    L = BLOCK

    @pl.when(c == 0)
    def _():
        c_ref[...] = jnp.zeros_like(c_ref)
        n_ref[...] = jnp.zeros_like(n_ref)
        m_ref[...] = jnp.zeros_like(m_ref)

    def conv_silu(p_ref, x_ref, off):
        cur = x_ref[0]
        prev = jnp.where(c > 0, p_ref[0], jnp.zeros_like(cur))
        sh = jnp.dot(sh_ref[...], jnp.concatenate([prev, cur], axis=0), preferred_element_type=F32)
        y = cb_ref[:, off:off + MIX_WIDTH]
        for j in range(CONV_WIDTH - 1):
            y = y + sh[j * L:(j + 1) * L] * cw_ref[j:j + 1, off:off + MIX_WIDTH]
        y = y + cur.astype(F32) * cw_ref[CONV_WIDTH - 1:CONV_WIDTH, off:off + MIX_WIDTH]
        return y * _sigmoid(y)

    qc = conv_silu(qp_ref, q_ref, 0)
    kc = conv_silu(kp_ref, k_ref, MIX_WIDTH) * (MLSTM_HEAD_DIM ** -0.5)
    gt = gt_ref[0]
    gT = gT_ref[0, 0]
    tri = lax.broadcasted_iota(jnp.int32, (L, L), 0) >= lax.broadcasted_iota(jnp.int32, (L, L), 1)

    for hd in range(MLSTM_HEADS):
        sl = slice(hd * MLSTM_HEAD_DIM, (hd + 1) * MLSTM_HEAD_DIM)
        q = qc[:, sl]
        k = kc[:, sl]
        qb, kb = q.astype(BF16), k.astype(BF16)
        v = v_ref[0, :, sl]
        i_col, b_col = gt[:, _GI + hd:_GI + hd + 1], gt[:, _GF + hd:_GF + hd + 1]
        i_row, b_row = gT[_GI + hd:_GI + hd + 1, :], gT[_GF + hd:_GF + hd + 1, :]
        b_last = b_col[L - 1:L, :]

        dmat = jnp.where(tri, b_col - b_row + i_row, NEG)
        m_loc = jnp.max(dmat, axis=1, keepdims=True)
        smat = jnp.exp(dmat - m_loc) * _nt_dot(qb, kb)
        s_v = jnp.dot(smat.astype(BF16), v, preferred_element_type=F32)
        s_sum = jnp.sum(smat, axis=1, keepdims=True)
        g_col = b_last - b_col + i_col
        g_max = jnp.max(b_last - b_row + i_row, axis=1, keepdims=True)
        kw = jnp.exp(g_col - g_max) * k
        kv_new = _tn_dot(kw.astype(BF16), v)
        k_new = jnp.sum(kw, axis=0, keepdims=True)

        m_prev = m_ref[hd]
        cmat = c_ref[hd]
        n_row = n_ref[hd]
        a_col = b_col + m_prev
        mt = jnp.maximum(a_col, m_loc)
        w_inter = jnp.exp(a_col - mt)
        w_intra = jnp.exp(m_loc - mt)
        num = w_inter * jnp.dot(qb, cmat.astype(BF16), preferred_element_type=F32) + w_intra * s_v
        den = w_inter * jnp.sum(q * n_row, axis=1, keepdims=True) + w_intra * s_sum
        hcur = num / jnp.maximum(jnp.abs(den), jnp.exp(-mt))

        m_new = jnp.maximum(b_last + m_prev, g_max)
        decay = jnp.exp(b_last + m_prev - m_new)
        w_new = jnp.exp(g_max - m_new)
        c_ref[hd] = decay * cmat + w_new * kv_new
        n_ref[hd] = decay * n_row + w_new * k_new
        m_ref[hd] = m_new

        hn = _rmsnorm_rows(hcur, on_ref[:, sl])
        y_ref[0, :, sl] = (hn * og_ref[0, :, sl].astype(F32)).astype(BF16)


def _mlstm(mq, mk, mv, og, gtok, gT, shifts, conv_w, conv_b, out_norm):
    B, S, _ = mq.shape
    cur = lambda b, c: (b, c, 0)
    prev = lambda b, c: (b, jnp.maximum(c - 1, 0), 0)
    tok = lambda w: pl.BlockSpec((1, BLOCK, w), cur)
    const = lambda shape: pl.BlockSpec(shape, lambda b, c: (0,) * len(shape))
    return pl.pallas_call(
        _mlstm_kernel,
        grid=(B, S // BLOCK),
        in_specs=[pl.BlockSpec((1, BLOCK, 512), prev), tok(512), pl.BlockSpec((1, BLOCK, 512), prev), tok(512),
                  tok(512), tok(512), tok(LANES),
                  pl.BlockSpec((1, 1, LANES, BLOCK), lambda b, c: (b, c, 0, 0)),
                  const(shifts.shape), const((CONV_WIDTH, 2 * MIX_WIDTH)), const((1, 2 * MIX_WIDTH)),
                  const((1, MIX_WIDTH))],
        out_specs=tok(512),
        out_shape=jax.ShapeDtypeStruct((B, S, MIX_WIDTH), BF16),
        scratch_shapes=[
            pltpu.VMEM((MLSTM_HEADS, MLSTM_HEAD_DIM, MLSTM_HEAD_DIM), F32),
            pltpu.VMEM((MLSTM_HEADS, 1, MLSTM_HEAD_DIM), F32),
            pltpu.VMEM((MLSTM_HEADS, 1, 1), F32),
        ],
        compiler_params=pltpu.CompilerParams(dimension_semantics=("parallel", "arbitrary"),
                                             vmem_limit_bytes=VMEM_LIMIT_BYTES),
        name="mlstm",
    )(mq, mq, mk, mk, mv, og, gtok, gT, shifts, conv_w, conv_b, out_norm)


def _merge_kernel(x_ref, g_ref, yf_ref, ys_ref, ym_ref, wgl_ref, wb_ref, wo_ref, o_ref):
    x = x_ref[...]
    h = _rmsnorm_rows(x, g_ref[...]).astype(BF16)
    merged = None
    for br, y_ref in enumerate((yf_ref, ys_ref, ym_ref)):
        gate = _sigmoid(jnp.dot(h, wgl_ref[:, br * D_MODEL:(br + 1) * D_MODEL], preferred_element_type=F32))
        term = gate * jnp.dot(y_ref[...], wb_ref[br], preferred_element_type=F32)
        merged = term if merged is None else merged + term
    o_ref[...] = x + jnp.dot(merged.astype(BF16), wo_ref[...], preferred_element_type=F32)


def _merge(x2, gain, yf, ys, ym, wgl, wb, wo, *, tm):
    T, D = x2.shape
    row = lambda w: pl.BlockSpec((tm, w), lambda t: (t, 0))
    const = lambda shape: pl.BlockSpec(shape, lambda t: (0,) * len(shape))
    return pl.pallas_call(
        _merge_kernel,
        grid=(T // tm,),
        in_specs=[row(D), const((1, D)), row(512), row(512), row(512),
                  const(wgl.shape), const(wb.shape), const(wo.shape)],
        out_specs=row(D),
        out_shape=jax.ShapeDtypeStruct((T, D), F32),
        compiler_params=pltpu.CompilerParams(dimension_semantics=("parallel",),
                                             vmem_limit_bytes=VMEM_LIMIT_BYTES),
        name="merge_out",
    )(x2, gain, yf, ys, ym, wgl, wb, wo)


def _mlp_kernel(x_ref, g_ref, wu_ref, wd_ref, o_ref, *, fc):
    x = x_ref[...]
    h = _rmsnorm_rows(x, g_ref[...]).astype(BF16)
    acc = x
    for f in range(D_FF // fc):
        a = jnp.maximum(jnp.dot(h, wu_ref[:, f * fc:(f + 1) * fc], preferred_element_type=F32), 0.0)
        acc = acc + jnp.dot((a * a).astype(BF16), wd_ref[f * fc:(f + 1) * fc, :], preferred_element_type=F32)
    o_ref[...] = acc


def _mlp(x2, gain, wu, wd, *, tm, fc):
    T, D = x2.shape
    row = pl.BlockSpec((tm, D), lambda t: (t, 0))
    const = lambda shape: pl.BlockSpec(shape, lambda t: (0,) * len(shape))
    return pl.pallas_call(
        functools.partial(_mlp_kernel, fc=fc),
        grid=(T // tm,),
        in_specs=[row, const((1, D)), const(wu.shape), const(wd.shape)],
        out_specs=row,
        out_shape=jax.ShapeDtypeStruct((T, D), F32),
        compiler_params=pltpu.CompilerParams(dimension_semantics=("parallel",),
                                             vmem_limit_bytes=VMEM_LIMIT_BYTES),
        name="mlp",
    )(x2, gain, wu, wd)


def _tiles(S):
    tm = 512 if S % 512 == 0 else S
    tq = 256 if S % 256 == 0 else S
    return tm, tq


def _layer_params(w_in, fox_f_bias, fox_q_norm, fox_k_norm, swa_q_norm, swa_k_norm, mlstm_i_bias, mlstm_f_bias):
    offs = np.cumsum(IN_SPLITS)[:-1].tolist()
    fq, fk, fv, ff, sq, sk, sv, mq, mk, mv, mi, mf, mo, gl = jnp.split(w_in, offs, axis=1)
    dup = lambda w: jnp.concatenate([w[:, :HEAD_DIM], w[:, :HEAD_DIM], w[:, HEAD_DIM:], w[:, HEAD_DIM:]], axis=1)

    def gate_group(fox, m_i, m_f):
        lead = fox.shape[:-1]
        fox8 = jnp.concatenate([fox.reshape(lead + (FOX_HEADS // 2, 2)),
                                jnp.zeros(lead + (FOX_HEADS // 2, 6), F32)], axis=-1).reshape(lead + (_GI,))
        return jnp.concatenate([fox8, m_i, m_f, jnp.zeros(lead + (LANES - _GF - MLSTM_HEADS,), F32)], axis=-1)

    gates = gate_group(ff, mi, mf)
    gbias = gate_group(fox_f_bias, mlstm_i_bias, mlstm_f_bias)[None, :]
    wt =jnp.concatenate([fk, sq, dup(sk), dup(sv), mq, mk, mv, mo, gates], axis=1).astype(BF16)
    wT = jnp.concatenate([fq, fv], axis=1).T.astype(BF16)
    fkg = jnp.tile(fox_k_norm * fox_q_norm * (HEAD_DIM ** -0.5 * LOG2E), FOX_HEADS)[None, :]
    sqg = jnp.tile(swa_q_norm * (HEAD_DIM ** -0.5), SWA_Q_HEADS)[None, :]
    skg = jnp.tile(swa_k_norm, 2 * SWA_KV_HEADS)[None, :]
    return wt, wT, fkg, sqg, skg, gbias, gl.astype(BF16)


def kernel(x, norm_mix, w_in, fox_f_bias, fox_q_norm, fox_k_norm, swa_q_norm, swa_k_norm, swa_sinks, conv_w,
           conv_b, mlstm_i_bias, mlstm_f_bias, mlstm_out_norm, w_branch, w_out, norm_mlp, w_up, w_down):
    B, S, D = x.shape
    depth = w_in.shape[0]
    tm, tq = _tiles(S)

    inv = ROPE_THETA ** (-jnp.arange(0, HEAD_DIM, 2, dtype=F32) / HEAD_DIM)
    ang = jnp.arange(S, dtype=F32)[:, None] * inv[None, :]
    cos, sin = jnp.cos(ang), jnp.sin(ang)
    cosf = jnp.tile(jnp.concatenate([cos, cos], axis=1), (1, LANES // HEAD_DIM))
    sinf = jnp.tile(jnp.concatenate([-sin, sin], axis=1), (1, LANES // HEAD_DIM))
    gm = jnp.asarray(np.kron(np.eye(MIX_WIDTH // HEAD_DIM), np.full((HEAD_DIM, HEAD_DIM), 1.0 / HEAD_DIM)), BF16)
    ltri = jnp.asarray(np.tril(np.ones((BLOCK, BLOCK))), BF16)
    shifts_np = np.zeros(((CONV_WIDTH - 1) * BLOCK, 2 * BLOCK))
    for j in range(CONV_WIDTH - 1):
        for t in range(BLOCK):
            shifts_np[j * BLOCK + t, BLOCK + t - (CONV_WIDTH - 1 - j)] = 1.0
    shifts = jnp.asarray(shifts_np, BF16)
    sel_np = np.zeros((FOX_BIAS_PIECES * LANES, MIX_WIDTH))
    for piece in range(FOX_BIAS_PIECES):
        for hd in range(FOX_HEADS):
            sel_np[piece * LANES + 8 * (hd // 2) + hd % 2, LANES * (hd // 2) + FOX_BIAS_PIECES * (hd % 2) + piece] = 1.0
    sel = jnp.asarray(sel_np, BF16)

    for l in range(depth):
        wt, wT, fkg, sqg, skg, gbias, wgl = _layer_params(
            w_in[l], fox_f_bias[l], fox_q_norm[l], fox_k_norm[l], swa_q_norm[l], swa_k_norm[l],
            mlstm_i_bias[l], mlstm_f_bias[l])
        (fqT, fvT, fk, sq, sk, sv, mq, mk, mv, og, gtok, gT, faug) = _in_proj(
            x, norm_mix[l][None, :], wt, wT, fkg, sqg, skg, gm, cosf, sinf, gbias, ltri, sel, tm=tm, tkv=tq)
        y_fox = _fox_attention(fqT, fk, fvT, faug, tq=tq)
        y_swa = _swa_attention(swa_sinks[l], sq, sk, sv)
        y_ml = _mlstm(mq, mk, mv, og, gtok, gT, shifts, conv_w[l], conv_b[l][None, :], mlstm_out_norm[l][None, :])
        x2 = _merge(x.reshape(B * S, D), norm_mix[l][None, :], y_fox.reshape(B * S, -1), y_swa.reshape(B * S, -1),
                    y_ml.reshape(B * S, -1), wgl, w_branch[l].astype(BF16), w_out[l].astype(BF16), tm=tm)
        x2 = _mlp(x2, norm_mlp[l][None, :], w_up[l].astype(BF16), w_down[l].astype(BF16), tm=tm, fc=1024)
        x = x2.reshape(B, S, D)
    return x
```

```python
import functools

import jax
import jax.numpy as jnp
import numpy as np
from jax import lax
from jax.experimental import pallas as pl
from jax.experimental.pallas import tpu as pltpu

F32 = jnp.float32
BF16 = jnp.bfloat16

D_MODEL = 1024
HEAD_DIM = 64
MIX_WIDTH = 512
FOX_HEADS = 8
SWA_Q_HEADS = 8
SWA_KV_HEADS = 2
SWA_WINDOW = 128
MLSTM_HEADS = 4
MLSTM_HEAD_DIM = 128
CONV_WIDTH = 4
D_FF = 4 * D_MODEL
N_BRANCHES = 3
BLOCK = 128
ROPE_THETA = 10000.0
EPS = 1e-6
NEG = -1e30
LOG2E = 1.4426950408889634
FOX_GROUP = 2

LANES = 128
FOX_VROWS = LANES + 16
FOX_BIAS_PIECES = 3
VMEM_LIMIT_BYTES = 56 * 1024 * 1024

_FK, _SQ, _SK, _SV, _MQ, _MK, _MV, _MO, _GT, _NTOK = 0, 512, 1024, 1280, 1536, 2048, 2560, 3072, 3584, 3712
_GI, _GF = 32, 36

IN_SPLITS = (512, 512, 512, 8, 512, 128, 128, 512, 512, 512, 4, 4, 512, 3 * D_MODEL)


def _nt_dot(a, b):
    return lax.dot_general(a, b, (((1,), (1,)), ((), ())), preferred_element_type=F32)


def _tn_dot(a, b):
    return lax.dot_general(a, b, (((0,), (0,)), ((), ())), preferred_element_type=F32)


def _sigmoid(x):
    return 1.0 / (1.0 + jnp.exp(-x))


def _log_sigmoid(x):
    return jnp.minimum(x, 0.0) - jnp.log(1.0 + jnp.exp(-jnp.abs(x)))


def _rmsnorm_rows(x, gain_row):
    ms = jnp.mean(x * x, axis=-1, keepdims=True)
    return x * lax.rsqrt(ms + EPS) * gain_row


def _in_proj_kernel(x_ref, g_ref, wt_ref, wT_ref, fkg_ref, sqg_ref, skg_ref, gm_ref, cos_ref, sin_ref,
                    gbias_ref, ltri_ref, sel_ref,
                    fqT_ref, fvT_ref, fk_ref, sq_ref, sk_ref, sv_ref, mq_ref, mk_ref, mv_ref, og_ref,
                    gtok_ref, gT_ref, aug_ref, carry_ref, *, tm, tkv):
    s = pl.program_id(1)
    h = _rmsnorm_rows(x_ref[0], g_ref[...]).astype(BF16)

    def proj(lo, hi):
        return jnp.dot(h, wt_ref[:, lo:hi], preferred_element_type=F32)

    def head_ms(u):
        w = u.shape[1]
        return jnp.dot((u * u).astype(BF16), gm_ref[:w, :w], preferred_element_type=F32)

    def rope(u):
        lane = lax.broadcasted_iota(jnp.int32, (tm, LANES), 1)
        first = (lane % HEAD_DIM) < (HEAD_DIM // 2)
        c, sn = cos_ref[...], sin_ref[...]
        outs = []
        for j in range(u.shape[1] // LANES):
            uj = u[:, j * LANES:(j + 1) * LANES]
            partner = jnp.where(first, pltpu.roll(uj, LANES - HEAD_DIM // 2, 1), pltpu.roll(uj, HEAD_DIM // 2, 1))
            outs.append(uj * c + partner * sn)
        return jnp.concatenate(outs, axis=1)

    qT = _nt_dot(wT_ref[0:MIX_WIDTH, :], h)
    msT = jnp.dot(gm_ref[...], (qT * qT).astype(BF16), preferred_element_type=F32)
    qn = (qT * lax.rsqrt(msT + EPS)).astype(BF16)
    pad = jnp.zeros((HEAD_DIM, tm), BF16)
    for hd in range(FOX_HEADS):
        rows = qn[hd * HEAD_DIM:(hd + 1) * HEAD_DIM]
        fqT_ref[0, hd] = jnp.concatenate([rows, pad] if hd % 2 == 0 else [pad, rows], axis=0)
    vT = _nt_dot(wT_ref[MIX_WIDTH:2 * MIX_WIDTH, :], h).astype(BF16)
    ones = jnp.ones((FOX_VROWS - LANES, tkv), BF16)
    for c in range(tm // tkv):
        for pr in range(FOX_HEADS // 2):
            fvT_ref[0, c, pr] = jnp.concatenate([vT[pr * LANES:(pr + 1) * LANES, c * tkv:(c + 1) * tkv], ones], axis=0)

    u = proj(_FK, _SQ)
    kn = (u * lax.rsqrt(head_ms(u) + EPS) * fkg_ref[...]).astype(BF16)
    for pr in range(FOX_HEADS // 2):
        fk_ref[0, pr] = kn[:, pr * LANES:(pr + 1) * LANES]

    u = proj(_SQ, _SK)
    sq_ref[0] = rope(u * lax.rsqrt(head_ms(u) + EPS) * sqg_ref[...]).astype(BF16)
    u = proj(_SK, _SV)
    sk_ref[0] = rope(u * lax.rsqrt(head_ms(u) + EPS) * skg_ref[...]).astype(BF16)
    sv_ref[0] = proj(_SV, _MQ).astype(BF16)

    mq_ref[0] = proj(_MQ, _MK).astype(BF16)
    mk_ref[0] = proj(_MK, _MV).astype(BF16)
    mv_ref[0] = proj(_MV, _MO).astype(BF16)
    og_ref[0] = _sigmoid(proj(_MO, _GT)).astype(BF16)

    @pl.when(s == 0)
    def _():
        carry_ref[...] = jnp.zeros_like(carry_ref)

    g = proj(_GT, _NTOK) + gbias_ref[...]
    col = lax.broadcasted_iota(jnp.int32, (BLOCK, LANES), 1)
    is_i = (col >= _GI) & (col < _GF)
    is_fox = col < _GI
    def bf16_pieces(v):
        pieces = []
        for _ in range(FOX_BIAS_PIECES):
            part = v.astype(BF16)
            pieces.append(part)
            v = v - part.astype(F32)
        return pieces

    nchunk = tm // BLOCK
    val_pieces = bf16_pieces(_log_sigmoid(g))
    rhs = jnp.concatenate([piece[c * BLOCK:(c + 1) * BLOCK] for c in range(nchunk) for piece in val_pieces], axis=1)
    cs_all = jnp.dot(ltri_ref[...], rhs, preferred_element_type=F32)
    carry = carry_ref[...]
    outs = []
    for c in range(nchunk):
        blocks = [cs_all[:, (FOX_BIAS_PIECES * c + q) * LANES:(FOX_BIAS_PIECES * c + q + 1) * LANES]
                  for q in range(FOX_BIAS_PIECES)]
        cs = blocks[0] + blocks[1] + blocks[2] + jnp.where(is_fox, carry, 0.0)
        out_c = jnp.where(is_i, g[c * BLOCK:(c + 1) * BLOCK], cs)
        carry = out_c[BLOCK - 1:BLOCK, :]
        gtok_ref[0, c * BLOCK:(c + 1) * BLOCK, :] = out_c
        gT_ref[0, c] = out_c.T
        outs.append(out_c)
    carry_ref[...] = carry
    aug_pieces = bf16_pieces(jnp.concatenate(outs, axis=0) * (-LOG2E))
    aug = jnp.dot(jnp.concatenate(aug_pieces, axis=1), sel_ref[...], preferred_element_type=F32).astype(BF16)
    for pr in range(FOX_HEADS // 2):
        aug_ref[0, pr] = aug[:, pr * LANES:(pr + 1) * LANES]


def _in_proj(x, gain, wt, wT, fkg, sqg, skg, gm, cosf, sinf, gbias, ltri, sel, *, tm, tkv):
    B, S, D = x.shape
    nc = S // BLOCK
    const = lambda shape: pl.BlockSpec(shape, lambda b, s: (0,) * len(shape))
    tok = lambda w: pl.BlockSpec((1, tm, w), lambda b, s: (b, s, 0))
    out_shape = (
        jax.ShapeDtypeStruct((B, FOX_HEADS, LANES, S), BF16),
        jax.ShapeDtypeStruct((B, S // tkv, FOX_HEADS // 2, FOX_VROWS, tkv), BF16),
        jax.ShapeDtypeStruct((B, FOX_HEADS // 2, S, LANES), BF16),
        jax.ShapeDtypeStruct((B, S, 512), BF16),
        jax.ShapeDtypeStruct((B, S, 256), BF16),
        jax.ShapeDtypeStruct((B, S, 256), BF16),
        jax.ShapeDtypeStruct((B, S, 512), BF16),
        jax.ShapeDtypeStruct((B, S, 512), BF16),
        jax.ShapeDtypeStruct((B, S, 512), BF16),
        jax.ShapeDtypeStruct((B, S, 512), BF16),
        jax.ShapeDtypeStruct((B, S, LANES), F32),
        jax.ShapeDtypeStruct((B, nc, LANES, BLOCK), F32),
        jax.ShapeDtypeStruct((B, FOX_HEADS // 2, S, LANES), BF16),
    )
    out_specs = (
        pl.BlockSpec((1, FOX_HEADS, LANES, tm), lambda b, s: (b, 0, 0, s)),
        pl.BlockSpec((1, tm // tkv, FOX_HEADS // 2, FOX_VROWS, tkv), lambda b, s: (b, s, 0, 0, 0)),
        pl.BlockSpec((1, FOX_HEADS // 2, tm, LANES), lambda b, s: (b, 0, s, 0)),
        tok(512), tok(256), tok(256), tok(512), tok(512), tok(512), tok(512), tok(LANES),
        pl.BlockSpec((1, tm // BLOCK, LANES, BLOCK), lambda b, s: (b, s, 0, 0)),
        pl.BlockSpec((1, FOX_HEADS // 2, tm, LANES), lambda b, s: (b, 0, s, 0)),
    )
    in_specs = [
        pl.BlockSpec((1, tm, D), lambda b, s: (b, s, 0)),
        const((1, D)), const(wt.shape), const(wT.shape), const((1, 512)), const((1, 512)), const((1, 256)),
        const((512, 512)),
        pl.BlockSpec((tm, LANES), lambda b, s: (s, 0)), pl.BlockSpec((tm, LANES), lambda b, s: (s, 0)),
        const((1, LANES)), const((BLOCK, BLOCK)), const(sel.shape),
    ]
    return pl.pallas_call(
        functools.partial(_in_proj_kernel, tm=tm, tkv=tkv),
        grid=(B, S // tm), in_specs=in_specs, out_specs=out_specs, out_shape=out_shape,
        scratch_shapes=[pltpu.VMEM((1, LANES), F32)],
        compiler_params=pltpu.CompilerParams(dimension_semantics=("parallel", "arbitrary"),
                                             vmem_limit_bytes=VMEM_LIMIT_BYTES),
        name="in_proj",
    )(x, gain, wt, wT, fkg, sqg, skg, gm, cosf, sinf, gbias, ltri, sel)


def _col_reduce(x, op, ways=8):
    rows, cols = x.shape
    return op(op(x.reshape(ways, rows // ways, cols), axis=0), axis=0, keepdims=True)


def _fox_kernel(qT_ref, k_ref, vT_ref, aug_ref, o_ref, qq_ref, m_ref, l_ref, acc_ref, za_ref, p_ref, al_ref, *, tq):
    i = pl.program_id(1)
    NP = FOX_HEADS // 2
    G = FOX_GROUP
    NAUG = FOX_BIAS_PIECES

    row = lax.broadcasted_iota(jnp.int32, (LANES, 2 * tq), 0)
    col = lax.broadcasted_iota(jnp.int32, (LANES, 2 * tq), 1)
    ones_rows = jnp.where((row < 2 * NAUG) & ((row >= NAUG) == (col >= tq)), 1.0, 0.0).astype(BF16)
    for pr in range(NP):
        qq_ref[pr] = jnp.concatenate(
            [jnp.concatenate([qT_ref[0, 2 * pr], qT_ref[0, 2 * pr + 1]], axis=1), ones_rows], axis=0)
    m_ref[...] = jnp.full(m_ref.shape, NEG, F32)
    l_ref[...] = jnp.zeros(l_ref.shape, F32)
    acc_ref[...] = jnp.zeros(acc_ref.shape, F32)
    za_ref[...] = jnp.full(za_ref.shape, NEG, F32)
    p_ref[...] = jnp.zeros(p_ref.shape, BF16)
    al_ref[...] = jnp.ones(al_ref.shape, F32)
    krow = lax.broadcasted_iota(jnp.int32, (tq, 2 * tq), 0)
    qcol = lax.broadcasted_iota(jnp.int32, (tq, 2 * tq), 1)
    causal = krow <= (qcol & (tq - 1))
    n_end = NP * (i + 1)

    def where_is(n):
        tile = jnp.clip(n >> 2, 0, i)
        return n & (NP - 1), tile, pl.multiple_of(tile * tq, tq)

    def stage_scores(n, j, guard):
        pr, _, r0 = where_is(n)
        kk = jnp.concatenate([k_ref[0, pr, pl.ds(r0, tq), :], aug_ref[0, pr, pl.ds(r0, tq), :]], axis=1)
        s = jnp.dot(kk, qq_ref[pr], preferred_element_type=F32)
        za_ref[j] = jnp.where(n < n_end, s, NEG) if guard else s

    def stage_softmax(n, j, masked):
        pr, _, _ = where_is(n)
        z = za_ref[j]
        if masked:
            z = jnp.where(causal, z, NEG)
        m = m_ref[pr]
        m_new = jnp.maximum(m, _col_reduce(z, jnp.max))
        p_ref[j] = jnp.exp2(z - m_new).astype(BF16)
        al_ref[j] = jnp.exp2(m - m_new)
        m_ref[pr] = m_new

    def stage_values(n, j):
        pr, tile, _ = where_is(n)
        pv = jnp.dot(vT_ref[0, tile, pr], p_ref[j], preferred_element_type=F32)
        al = al_ref[j]
        acc_ref[2 * pr] = al[:, :tq] * acc_ref[2 * pr] + pv[:HEAD_DIM, :tq]
        acc_ref[2 * pr + 1] = al[:, tq:] * acc_ref[2 * pr + 1] + pv[HEAD_DIM:2 * HEAD_DIM, tq:]
        l_ref[pr] = al * l_ref[pr] + pv[2 * HEAD_DIM:2 * HEAD_DIM + 1, :]

    def step(g, masked, guard):
        for j in range(G):
            stage_values((g - 2) * G + j, j)
        for j in range(G):
            stage_softmax((g - 1) * G + j, j, masked)
        for j in range(G):
            stage_scores(g * G + j, j, guard)

    def body_plain(g, carry):
        step(g, False, False)
        return carry

    def body_diag(g, carry):
        step(g, True, False)
        return carry

    g_diag = (NP // G) * i + 1
    g_end = (NP // G) * (i + 1)
    if NP // G == 2:
        def body_pair(k, carry):
            step(2 * k, False, False)
            step(2 * k + 1, False, False)
            return carry

        lax.fori_loop(0, i, body_pair, 0)
        step(2 * i, False, False)
        step(2 * i + 1, True, False)
    else:
        lax.fori_loop(0, g_diag, body_plain, 0)
        lax.fori_loop(g_diag, g_end, body_diag, 0)
    for j in range(G):
        stage_values((g_end - 2) * G + j, j)
    for j in range(G):
        stage_softmax((g_end - 1) * G + j, j, True)
    for j in range(G):
        stage_values((g_end - 1) * G + j, j)
    for pr in range(NP):
        la, lb = l_ref[pr][:, :tq], l_ref[pr][:, tq:]
        oT = jnp.concatenate([acc_ref[2 * pr] / la, acc_ref[2 * pr + 1] / lb], axis=0)
        o_ref[0, :, pr * LANES:(pr + 1) * LANES] = oT.T.astype(BF16)


def _fox_attention(fqT, fk, fvT, faug, *, tq):
    B, NP, S, _ = fk.shape
    nkv = S // tq
    vrows = fvT.shape[3]
    return pl.pallas_call(
        functools.partial(_fox_kernel, tq=tq),
        grid=(B, S // tq),
        in_specs=[
            pl.BlockSpec((1, FOX_HEADS, LANES, tq), lambda b, i: (b, 0, 0, i)),
            pl.BlockSpec((1, NP, S, LANES), lambda b, i: (b, 0, 0, 0)),
            pl.BlockSpec((1, nkv, NP, vrows, tq), lambda b, i: (b, 0, 0, 0, 0)),
            pl.BlockSpec((1, NP, S, LANES), lambda b, i: (b, 0, 0, 0)),
        ],
        out_specs=pl.BlockSpec((1, tq, MIX_WIDTH), lambda b, i: (b, i, 0)),
        out_shape=jax.ShapeDtypeStruct((B, S, MIX_WIDTH), BF16),
        scratch_shapes=[
            pltpu.VMEM((NP, 2 * LANES, 2 * tq), BF16),
            pltpu.VMEM((NP, 1, 2 * tq), F32),
            pltpu.VMEM((NP, 1, 2 * tq), F32),
            pltpu.VMEM((FOX_HEADS, HEAD_DIM, tq), F32),
            pltpu.VMEM((FOX_GROUP, tq, 2 * tq), F32),
            pltpu.VMEM((FOX_GROUP, tq, 2 * tq), BF16),
            pltpu.VMEM((FOX_GROUP, 1, 2 * tq), F32),
        ],
        compiler_params=pltpu.CompilerParams(dimension_semantics=("parallel", "arbitrary"),
                                             vmem_limit_bytes=VMEM_LIMIT_BYTES),
        name="fox_attention",
    )(fqT, fk, fvT, faug)


def _swa_kernel(sinks_ref, q_ref, kp_ref, kc_ref, vp_ref, vc_ref, o_ref):
    n = pl.program_id(1)
    L = BLOCK
    row = lax.broadcasted_iota(jnp.int32, (L, 2 * L), 0)
    col = lax.broadcasted_iota(jnp.int32, (L, 2 * L), 1)
    row_prev = row + jnp.where(n == 0, 2 * L, 0)
    mask = ((col < L) & (col > row_prev)) | ((col >= L) & ((col - L) <= row))
    lane = lax.broadcasted_iota(jnp.int32, (L, LANES), 1)
    low = lane < HEAD_DIM
    q = q_ref[0]
    zero = jnp.zeros((L, LANES), BF16)
    for pair in range(SWA_Q_HEADS // 2):
        g = pair // 2
        kf = jnp.concatenate([kp_ref[0, :, g * LANES:(g + 1) * LANES], kc_ref[0, :, g * LANES:(g + 1) * LANES]], axis=0)
        vf = jnp.concatenate([vp_ref[0, :, g * LANES:(g + 1) * LANES], vc_ref[0, :, g * LANES:(g + 1) * LANES]], axis=0)
        qp = q[:, pair * LANES:(pair + 1) * LANES]
        outs = []
        for half in range(2):
            sink = sinks_ref[2 * pair + half]
            qh = jnp.where(low if half == 0 else ~low, qp, zero)
            z = jnp.where(mask, _nt_dot(qh, kf), NEG)
            m = jnp.maximum(jnp.max(z, axis=1, keepdims=True), sink)
            pm = jnp.exp(z - m)
            den = jnp.sum(pm, axis=1, keepdims=True) + jnp.exp(sink - m)
            outs.append(jnp.dot(pm.astype(BF16), vf, preferred_element_type=F32) / den)
        o_ref[0, :, pair * LANES:(pair + 1) * LANES] = jnp.where(low, outs[0], outs[1]).astype(BF16)


def _swa_attention(sinks, sq, sk, sv):
    B, S, _ = sq.shape
    prev = lambda b, n: (b, jnp.maximum(n - 1, 0), 0)
    cur = lambda b, n: (b, n, 0)
    return pl.pallas_call(
        _swa_kernel,
        grid=(B, S // BLOCK),
        in_specs=[
            pl.BlockSpec(memory_space=pltpu.SMEM),
            pl.BlockSpec((1, BLOCK, 512), cur),
            pl.BlockSpec((1, BLOCK, 256), prev), pl.BlockSpec((1, BLOCK, 256), cur),
            pl.BlockSpec((1, BLOCK, 256), prev), pl.BlockSpec((1, BLOCK, 256), cur),
        ],
        out_specs=pl.BlockSpec((1, BLOCK, 512), cur),
        out_shape=jax.ShapeDtypeStruct((B, S, MIX_WIDTH), BF16),
        compiler_params=pltpu.CompilerParams(dimension_semantics=("parallel", "parallel"),
                                             vmem_limit_bytes=VMEM_LIMIT_BYTES),
        name="swa_attention",
    )(sinks, sq, sk, sk, sv, sv)


def _mlstm_kernel(qp_ref, q_ref, kp_ref, k_ref, v_ref, og_ref, gt_ref, gT_ref, sh_ref, cw_ref, cb_ref, on_ref,
                  y_ref, c_ref, n_ref, m_ref):
    c = pl.program_id(1)
    L = BLOCK

    @pl.when(c == 0)
    def _():
        c_ref[...] = jnp.zeros_like(c_ref)
        n_ref[...] = jnp.zeros_like(n_ref)
        m_ref[...] = jnp.zeros_like(m_ref)

    def conv_silu(p_ref, x_ref, off):
        cur = x_ref[0]
        prev = jnp.where(c > 0, p_ref[0], jnp.zeros_like(cur))
        sh = jnp.dot(sh_ref[...], jnp.concatenate([prev, cur], axis=0), preferred_element_type=F32)
        y = cb_ref[:, off:off + MIX_WIDTH]
        for j in range(CONV_WIDTH - 1):
            y = y + sh[j * L:(j + 1) * L] * cw_ref[j:j + 1, off:off + MIX_WIDTH]
        y = y + cur.astype(F32) * cw_ref[CONV_WIDTH - 1:CONV_WIDTH, off:off + MIX_WIDTH]
        return y * _sigmoid(y)

    qc = conv_silu(qp_ref, q_ref, 0)
    kc = conv_silu(kp_ref, k_ref, MIX_WIDTH) * (MLSTM_HEAD_DIM ** -0.5)
    gt = gt_ref[0]
    gT = gT_ref[0, 0]
    tri = lax.broadcasted_iota(jnp.int32, (L, L), 0) >= lax.broadcasted_iota(jnp.int32, (L, L), 1)

    for hd in range(MLSTM_HEADS):
        sl = slice(hd * MLSTM_HEAD_DIM, (hd + 1) * MLSTM_HEAD_DIM)
        q = qc[:, sl]
        k = kc[:, sl]
        qb, kb = q.astype(BF16), k.astype(BF16)
        v = v_ref[0, :, sl]
        i_col, b_col = gt[:, _GI + hd:_GI + hd + 1], gt[:, _GF + hd:_GF + hd + 1]
        i_row, b_row = gT[_GI + hd:_GI + hd + 1, :], gT[_GF + hd:_GF + hd + 1, :]
        b_last = b_col[L - 1:L, :]

        dmat = jnp.where(tri, b_col - b_row + i_row, NEG)
        m_loc = jnp.max(dmat, axis=1, keepdims=True)
        smat = jnp.exp(dmat - m_loc) * _nt_dot(qb, kb)
        s_v = jnp.dot(smat.astype(BF16), v, preferred_element_type=F32)
        s_sum = jnp.sum(smat, axis=1, keepdims=True)
        g_col = b_last - b_col + i_col
        g_max = jnp.max(b_last - b_row + i_row, axis=1, keepdims=True)
        kw = jnp.exp(g_col - g_max) * k
        kv_new = _tn_dot(kw.astype(BF16), v)
        k_new = jnp.sum(kw, axis=0, keepdims=True)

        m_prev = m_ref[hd]
        cmat = c_ref[hd]
        n_row = n_ref[hd]
        a_col = b_col + m_prev
        mt = jnp.maximum(a_col, m_loc)
        w_inter = jnp.exp(a_col - mt)
        w_intra = jnp.exp(m_loc - mt)
        num = w_inter * jnp.dot(qb, cmat.astype(BF16), preferred_element_type=F32) + w_intra * s_v
        den = w_inter * jnp.sum(q * n_row, axis=1, keepdims=True) + w_intra * s_sum
        hcur = num / jnp.maximum(jnp.abs(den), jnp.exp(-mt))

        m_new = jnp.maximum(b_last + m_prev, g_max)
        decay = jnp.exp(b_last + m_prev - m_new)
        w_new = jnp.exp(g_max - m_new)
        c_ref[hd] = decay * cmat + w_new * kv_new
        n_ref[hd] = decay * n_row + w_new * k_new
        m_ref[hd] = m_new

        hn = _rmsnorm_rows(hcur, on_ref[:, sl])
        y_ref[0, :, sl] = (hn * og_ref[0, :, sl].astype(F32)).astype(BF16)


def _mlstm(mq, mk, mv, og, gtok, gT, shifts, conv_w, conv_b, out_norm):
    B, S, _ = mq.shape
    cur = lambda b, c: (b, c, 0)
    prev = lambda b, c: (b, jnp.maximum(c - 1, 0), 0)
    tok = lambda w: pl.BlockSpec((1, BLOCK, w), cur)
    const = lambda shape: pl.BlockSpec(shape, lambda b, c: (0,) * len(shape))
    return pl.pallas_call(
        _mlstm_kernel,
        grid=(B, S // BLOCK),
        in_specs=[pl.BlockSpec((1, BLOCK, 512), prev), tok(512), pl.BlockSpec((1, BLOCK, 512), prev), tok(512),
                  tok(512), tok(512), tok(LANES),
                  pl.BlockSpec((1, 1, LANES, BLOCK), lambda b, c: (b, c, 0, 0)),
                  const(shifts.shape), const((CONV_WIDTH, 2 * MIX_WIDTH)), const((1, 2 * MIX_WIDTH)),
                  const((1, MIX_WIDTH))],
        out_specs=tok(512),
        out_shape=jax.ShapeDtypeStruct((B, S, MIX_WIDTH), BF16),
        scratch_shapes=[
            pltpu.VMEM((MLSTM_HEADS, MLSTM_HEAD_DIM, MLSTM_HEAD_DIM), F32),
            pltpu.VMEM((MLSTM_HEADS, 1, MLSTM_HEAD_DIM), F32),
            pltpu.VMEM((MLSTM_HEADS, 1, 1), F32),
        ],
        compiler_params=pltpu.CompilerParams(dimension_semantics=("parallel", "arbitrary"),
                                             vmem_limit_bytes=VMEM_LIMIT_BYTES),
        name="mlstm",
    )(mq, mq, mk, mk, mv, og, gtok, gT, shifts, conv_w, conv_b, out_norm)


def _merge_kernel(x_ref, g_ref, yf_ref, ys_ref, ym_ref, wgl_ref, wb_ref, wo_ref, o_ref):
    x = x_ref[...]
    h = _rmsnorm_rows(x, g_ref[...]).astype(BF16)
    merged = None
    for br, y_ref in enumerate((yf_ref, ys_ref, ym_ref)):
        gate = _sigmoid(jnp.dot(h, wgl_ref[:, br * D_MODEL:(br + 1) * D_MODEL], preferred_element_type=F32))
        term = gate * jnp.dot(y_ref[...], wb_ref[br], preferred_element_type=F32)
        merged = term if merged is None else merged + term
    o_ref[...] = x + jnp.dot(merged.astype(BF16), wo_ref[...], preferred_element_type=F32)


def _merge(x2, gain, yf, ys, ym, wgl, wb, wo, *, tm):
    T, D = x2.shape
    row = lambda w: pl.BlockSpec((tm, w), lambda t: (t, 0))
    const = lambda shape: pl.BlockSpec(shape, lambda t: (0,) * len(shape))
    return pl.pallas_call(
        _merge_kernel,
        grid=(T // tm,),
        in_specs=[row(D), const((1, D)), row(512), row(512), row(512),
                  const(wgl.shape), const(wb.shape), const(wo.shape)],
        out_specs=row(D),
        out_shape=jax.ShapeDtypeStruct((T, D), F32),
        compiler_params=pltpu.CompilerParams(dimension_semantics=("parallel",),
                                             vmem_limit_bytes=VMEM_LIMIT_BYTES),
        name="merge_out",
    )(x2, gain, yf, ys, ym, wgl, wb, wo)


def _mlp_kernel(x_ref, g_ref, wu_ref, wd_ref, o_ref, *, fc):
    x = x_ref[...]
    h = _rmsnorm_rows(x, g_ref[...]).astype(BF16)
    acc = x
    for f in range(D_FF // fc):
        a = jnp.maximum(jnp.dot(h, wu_ref[:, f * fc:(f + 1) * fc], preferred_element_type=F32), 0.0)
        acc = acc + jnp.dot((a * a).astype(BF16), wd_ref[f * fc:(f + 1) * fc, :], preferred_element_type=F32)
    o_ref[...] = acc


def _mlp(x2, gain, wu, wd, *, tm, fc):
    T, D = x2.shape
    row = pl.BlockSpec((tm, D), lambda t: (t, 0))
    const = lambda shape: pl.BlockSpec(shape, lambda t: (0,) * len(shape))
    return pl.pallas_call(
        functools.partial(_mlp_kernel, fc=fc),
        grid=(T // tm,),
        in_specs=[row, const((1, D)), const(wu.shape), const(wd.shape)],
        out_specs=row,
        out_shape=jax.ShapeDtypeStruct((T, D), F32),
        compiler_params=pltpu.CompilerParams(dimension_semantics=("parallel",),
                                             vmem_limit_bytes=VMEM_LIMIT_BYTES),
        name="mlp",
    )(x2, gain, wu, wd)


def _tiles(S):
    tm = 512 if S % 512 == 0 else S
    tq = 256 if S % 256 == 0 else S
    return tm, tq


def _layer_params(w_in, fox_f_bias, fox_q_norm, fox_k_norm, swa_q_norm, swa_k_norm, mlstm_i_bias, mlstm_f_bias):
    offs = np.cumsum(IN_SPLITS)[:-1].tolist()
    fq, fk, fv, ff, sq, sk, sv, mq, mk, mv, mi, mf, mo, gl = jnp.split(w_in, offs, axis=1)
    dup = lambda w: jnp.concatenate([w[:, :HEAD_DIM], w[:, :HEAD_DIM], w[:, HEAD_DIM:], w[:, HEAD_DIM:]], axis=1)

    def gate_group(fox, m_i, m_f):
        lead = fox.shape[:-1]
        fox8 = jnp.concatenate([fox.reshape(lead + (FOX_HEADS // 2, 2)),
                                jnp.zeros(lead + (FOX_HEADS // 2, 6), F32)], axis=-1).reshape(lead + (_GI,))
        return jnp.concatenate([fox8, m_i, m_f, jnp.zeros(lead + (LANES - _GF - MLSTM_HEADS,), F32)], axis=-1)

    gates = gate_group(ff, mi, mf)
    gbias = gate_group(fox_f_bias, mlstm_i_bias, mlstm_f_bias)[None, :]
    wt =jnp.concatenate([fk, sq, dup(sk), dup(sv), mq, mk, mv, mo, gates], axis=1).astype(BF16)
    wT = jnp.concatenate([fq, fv], axis=1).T.astype(BF16)
    fkg = jnp.tile(fox_k_norm * fox_q_norm * (HEAD_DIM ** -0.5 * LOG2E), FOX_HEADS)[None, :]
    sqg = jnp.tile(swa_q_norm * (HEAD_DIM ** -0.5), SWA_Q_HEADS)[None, :]
    skg = jnp.tile(swa_k_norm, 2 * SWA_KV_HEADS)[None, :]
    return wt, wT, fkg, sqg, skg, gbias, gl.astype(BF16)


def kernel(x, norm_mix, w_in, fox_f_bias, fox_q_norm, fox_k_norm, swa_q_norm, swa_k_norm, swa_sinks, conv_w,
           conv_b, mlstm_i_bias, mlstm_f_bias, mlstm_out_norm, w_branch, w_out, norm_mlp, w_up, w_down):
    B, S, D = x.shape
    depth = w_in.shape[0]
    tm, tq = _tiles(S)

    inv = ROPE_THETA ** (-jnp.arange(0, HEAD_DIM, 2, dtype=F32) / HEAD_DIM)
    ang = jnp.arange(S, dtype=F32)[:, None] * inv[None, :]
    cos, sin = jnp.cos(ang), jnp.sin(ang)
    cosf = jnp.tile(jnp.concatenate([cos, cos], axis=1), (1, LANES // HEAD_DIM))
    sinf = jnp.tile(jnp.concatenate([-sin, sin], axis=1), (1, LANES // HEAD_DIM))
    gm = jnp.asarray(np.kron(np.eye(MIX_WIDTH // HEAD_DIM), np.full((HEAD_DIM, HEAD_DIM), 1.0 / HEAD_DIM)), BF16)
    ltri = jnp.asarray(np.tril(np.ones((BLOCK, BLOCK))), BF16)
    shifts_np = np.zeros(((CONV_WIDTH - 1) * BLOCK, 2 * BLOCK))
    for j in range(CONV_WIDTH - 1):
        for t in range(BLOCK):
            shifts_np[j * BLOCK + t, BLOCK + t - (CONV_WIDTH - 1 - j)] = 1.0
    shifts = jnp.asarray(shifts_np, BF16)
    sel_np = np.zeros((FOX_BIAS_PIECES * LANES, MIX_WIDTH))
    for piece in range(FOX_BIAS_PIECES):
        for hd in range(FOX_HEADS):
            sel_np[piece * LANES + 8 * (hd // 2) + hd % 2, LANES * (hd // 2) + FOX_BIAS_PIECES * (hd % 2) + piece] = 1.0
    sel = jnp.asarray(sel_np, BF16)

    for l in range(depth):
        wt, wT, fkg, sqg, skg, gbias, wgl = _layer_params(
            w_in[l], fox_f_bias[l], fox_q_norm[l], fox_k_norm[l], swa_q_norm[l], swa_k_norm[l],
            mlstm_i_bias[l], mlstm_f_bias[l])
        (fqT, fvT, fk, sq, sk, sv, mq, mk, mv, og, gtok, gT, faug) = _in_proj(
            x, norm_mix[l][None, :], wt, wT, fkg, sqg, skg, gm, cosf, sinf, gbias, ltri, sel, tm=tm, tkv=tq)
        y_fox = _fox_attention(fqT, fk, fvT, faug, tq=tq)
        y_swa = _swa_attention(swa_sinks[l], sq, sk, sv)
        y_ml = _mlstm(mq, mk, mv, og, gtok, gT, shifts, conv_w[l], conv_b[l][None, :], mlstm_out_norm[l][None, :])
        x2 = _merge(x.reshape(B * S, D), norm_mix[l][None, :], y_fox.reshape(B * S, -1), y_swa.reshape(B * S, -1),
                    y_ml.reshape(B * S, -1), wgl, w_branch[l].astype(BF16), w_out[l].astype(BF16), tm=tm)
        x2 = _mlp(x2, norm_mlp[l][None, :], w_up[l].astype(BF16), w_down[l].astype(BF16), tm=tm, fc=1024)
        x = x2.reshape(B, S, D)
    return x
```

```python
import functools

import jax
import jax.numpy as jnp
import numpy as np
from jax import lax
from jax.experimental import pallas as pl
from jax.experimental.pallas import tpu as pltpu

F32 = jnp.float32
BF16 = jnp.bfloat16

D_MODEL = 1024
HEAD_DIM = 64
MIX_WIDTH = 512
FOX_HEADS = 8
SWA_Q_HEADS = 8
SWA_KV_HEADS = 2
SWA_WINDOW = 128
MLSTM_HEADS = 4
MLSTM_HEAD_DIM = 128
CONV_WIDTH = 4
D_FF = 4 * D_MODEL
N_BRANCHES = 3
BLOCK = 128
ROPE_THETA = 10000.0
EPS = 1e-6
NEG = -1e30
LOG2E = 1.4426950408889634
FOX_GROUP = 2

LANES = 128
FOX_VROWS = LANES + 16
FOX_BIAS_PIECES = 3
VMEM_LIMIT_BYTES = 56 * 1024 * 1024

_FK, _SQ, _SK, _SV, _MQ, _MK, _MV, _MO, _GT, _NTOK = 0, 512, 1024, 1280, 1536, 2048, 2560, 3072, 3584, 3712
_GI, _GF = 32, 36

IN_SPLITS = (512, 512, 512, 8, 512, 128, 128, 512, 512, 512, 4, 4, 512, 3 * D_MODEL)


def _nt_dot(a, b):
    return lax.dot_general(a, b, (((1,), (1,)), ((), ())), preferred_element_type=F32)


def _tn_dot(a, b):
    return lax.dot_general(a, b, (((0,), (0,)), ((), ())), preferred_element_type=F32)


def _sigmoid(x):
    return 1.0 / (1.0 + jnp.exp(-x))


def _log_sigmoid(x):
    return jnp.minimum(x, 0.0) - jnp.log(1.0 + jnp.exp(-jnp.abs(x)))


def _rmsnorm_rows(x, gain_row):
    ms = jnp.mean(x * x, axis=-1, keepdims=True)
    return x * lax.rsqrt(ms + EPS) * gain_row


def _in_proj_kernel(x_ref, g_ref, wt_ref, wT_ref, fkg_ref, sqg_ref, skg_ref, gm_ref, cos_ref, sin_ref,
                    gbias_ref, ltri_ref, sel_ref,
                    fqT_ref, fvT_ref, fk_ref, sq_ref, sk_ref, sv_ref, mq_ref, mk_ref, mv_ref, og_ref,
                    gtok_ref, gT_ref, aug_ref, carry_ref, *, tm, tkv):
    s = pl.program_id(1)
    h = _rmsnorm_rows(x_ref[0], g_ref[...]).astype(BF16)

    def proj(lo, hi):
        return jnp.dot(h, wt_ref[:, lo:hi], preferred_element_type=F32)

    def head_ms(u):
        w = u.shape[1]
        return jnp.dot((u * u).astype(BF16), gm_ref[:w, :w], preferred_element_type=F32)

    def rope(u):
        lane = lax.broadcasted_iota(jnp.int32, (tm, LANES), 1)
        first = (lane % HEAD_DIM) < (HEAD_DIM // 2)
        c, sn = cos_ref[...], sin_ref[...]
        outs = []
        for j in range(u.shape[1] // LANES):
            uj = u[:, j * LANES:(j + 1) * LANES]
            partner = jnp.where(first, pltpu.roll(uj, LANES - HEAD_DIM // 2, 1), pltpu.roll(uj, HEAD_DIM // 2, 1))
            outs.append(uj * c + partner * sn)
        return jnp.concatenate(outs, axis=1)

    qT = _nt_dot(wT_ref[0:MIX_WIDTH, :], h)
    msT = jnp.dot(gm_ref[...], (qT * qT).astype(BF16), preferred_element_type=F32)
    qn = (qT * lax.rsqrt(msT + EPS)).astype(BF16)
    pad = jnp.zeros((HEAD_DIM, tm), BF16)
    for hd in range(FOX_HEADS):
        rows = qn[hd * HEAD_DIM:(hd + 1) * HEAD_DIM]
        fqT_ref[0, hd] = jnp.concatenate([rows, pad] if hd % 2 == 0 else [pad, rows], axis=0)
    vT = _nt_dot(wT_ref[MIX_WIDTH:2 * MIX_WIDTH, :], h).astype(BF16)
    ones = jnp.ones((FOX_VROWS - LANES, tkv), BF16)
    for c in range(tm // tkv):
        for pr in range(FOX_HEADS // 2):
            fvT_ref[0, c, pr] = jnp.concatenate([vT[pr * LANES:(pr + 1) * LANES, c * tkv:(c + 1) * tkv], ones], axis=0)

    u = proj(_FK, _SQ)
    kn = (u * lax.rsqrt(head_ms(u) + EPS) * fkg_ref[...]).astype(BF16)
    for pr in range(FOX_HEADS // 2):
        fk_ref[0, pr] = kn[:, pr * LANES:(pr + 1) * LANES]

    u = proj(_SQ, _SK)
    sq_ref[0] = rope(u * lax.rsqrt(head_ms(u) + EPS) * sqg_ref[...]).astype(BF16)
    u = proj(_SK, _SV)
    sk_ref[0] = rope(u * lax.rsqrt(head_ms(u) + EPS) * skg_ref[...]).astype(BF16)
    sv_ref[0] = proj(_SV, _MQ).astype(BF16)

    mq_ref[0] = proj(_MQ, _MK).astype(BF16)
    mk_ref[0] = proj(_MK, _MV).astype(BF16)
    mv_ref[0] = proj(_MV, _MO).astype(BF16)
    og_ref[0] = _sigmoid(proj(_MO, _GT)).astype(BF16)

    @pl.when(s == 0)
    def _():
        carry_ref[...] = jnp.zeros_like(carry_ref)

    g = proj(_GT, _NTOK) + gbias_ref[...]
    col = lax.broadcasted_iota(jnp.int32, (BLOCK, LANES), 1)
    is_i = (col >= _GI) & (col < _GF)
    is_fox = col < _GI
    def bf16_pieces(v):
        pieces = []
        for _ in range(FOX_BIAS_PIECES):
            part = v.astype(BF16)
            pieces.append(part)
            v = v - part.astype(F32)
        return pieces

    nchunk = tm // BLOCK
    val_pieces = bf16_pieces(_log_sigmoid(g))
    rhs = jnp.concatenate([piece[c * BLOCK:(c + 1) * BLOCK] for c in range(nchunk) for piece in val_pieces], axis=1)
    cs_all = jnp.dot(ltri_ref[...], rhs, preferred_element_type=F32)
    carry = carry_ref[...]
    outs = []
    for c in range(nchunk):
        blocks = [cs_all[:, (FOX_BIAS_PIECES * c + q) * LANES:(FOX_BIAS_PIECES * c + q + 1) * LANES]
                  for q in range(FOX_BIAS_PIECES)]
        cs = blocks[0] + blocks[1] + blocks[2] + jnp.where(is_fox, carry, 0.0)
        out_c = jnp.where(is_i, g[c * BLOCK:(c + 1) * BLOCK], cs)
        carry = out_c[BLOCK - 1:BLOCK, :]
        gtok_ref[0, c * BLOCK:(c + 1) * BLOCK, :] = out_c
        gT_ref[0, c] = out_c.T
        outs.append(out_c)
    carry_ref[...] = carry
    aug_pieces = bf16_pieces(jnp.concatenate(outs, axis=0) * (-LOG2E))
    aug = jnp.dot(jnp.concatenate(aug_pieces, axis=1), sel_ref[...], preferred_element_type=F32).astype(BF16)
    for pr in range(FOX_HEADS // 2):
        aug_ref[0, pr] = aug[:, pr * LANES:(pr + 1) * LANES]


def _in_proj(x, gain, wt, wT, fkg, sqg, skg, gm, cosf, sinf, gbias, ltri, sel, *, tm, tkv):
    B, S, D = x.shape
    nc = S // BLOCK
    const = lambda shape: pl.BlockSpec(shape, lambda b, s: (0,) * len(shape))
    tok = lambda w: pl.BlockSpec((1, tm, w), lambda b, s: (b, s, 0))
    out_shape = (
        jax.ShapeDtypeStruct((B, FOX_HEADS, LANES, S), BF16),
        jax.ShapeDtypeStruct((B, S // tkv, FOX_HEADS // 2, FOX_VROWS, tkv), BF16),
        jax.ShapeDtypeStruct((B, FOX_HEADS // 2, S, LANES), BF16),
        jax.ShapeDtypeStruct((B, S, 512), BF16),
        jax.ShapeDtypeStruct((B, S, 256), BF16),
        jax.ShapeDtypeStruct((B, S, 256), BF16),
        jax.ShapeDtypeStruct((B, S, 512), BF16),
        jax.ShapeDtypeStruct((B, S, 512), BF16),
        jax.ShapeDtypeStruct((B, S, 512), BF16),
        jax.ShapeDtypeStruct((B, S, 512), BF16),
        jax.ShapeDtypeStruct((B, S, LANES), F32),
        jax.ShapeDtypeStruct((B, nc, LANES, BLOCK), F32),
        jax.ShapeDtypeStruct((B, FOX_HEADS // 2, S, LANES), BF16),
    )
    out_specs = (
        pl.BlockSpec((1, FOX_HEADS, LANES, tm), lambda b, s: (b, 0, 0, s)),
        pl.BlockSpec((1, tm // tkv, FOX_HEADS // 2, FOX_VROWS, tkv), lambda b, s: (b, s, 0, 0, 0)),
        pl.BlockSpec((1, FOX_HEADS // 2, tm, LANES), lambda b, s: (b, 0, s, 0)),
        tok(512), tok(256), tok(256), tok(512), tok(512), tok(512), tok(512), tok(LANES),
        pl.BlockSpec((1, tm // BLOCK, LANES, BLOCK), lambda b, s: (b, s, 0, 0)),
        pl.BlockSpec((1, FOX_HEADS // 2, tm, LANES), lambda b, s: (b, 0, s, 0)),
    )
    in_specs = [
        pl.BlockSpec((1, tm, D), lambda b, s: (b, s, 0)),
        const((1, D)), const(wt.shape), const(wT.shape), const((1, 512)), const((1, 512)), const((1, 256)),
        const((512, 512)),
        pl.BlockSpec((tm, LANES), lambda b, s: (s, 0)), pl.BlockSpec((tm, LANES), lambda b, s: (s, 0)),
        const((1, LANES)), const((BLOCK, BLOCK)), const(sel.shape),
    ]
    return pl.pallas_call(
        functools.partial(_in_proj_kernel, tm=tm, tkv=tkv),
        grid=(B, S // tm), in_specs=in_specs, out_specs=out_specs, out_shape=out_shape,
        scratch_shapes=[pltpu.VMEM((1, LANES), F32)],
        compiler_params=pltpu.CompilerParams(dimension_semantics=("parallel", "arbitrary"),
                                             vmem_limit_bytes=VMEM_LIMIT_BYTES),
        name="in_proj",
    )(x, gain, wt, wT, fkg, sqg, skg, gm, cosf, sinf, gbias, ltri, sel)


def _col_reduce(x, op, ways=8):
    rows, cols = x.shape
    return op(op(x.reshape(ways, rows // ways, cols), axis=0), axis=0, keepdims=True)


def _fox_kernel(qT_ref, k_ref, vT_ref, aug_ref, o_ref, qq_ref, m_ref, l_ref, acc_ref, za_ref, p_ref, al_ref, *, tq):
    i = pl.program_id(1)
    NP = FOX_HEADS // 2
    G = FOX_GROUP
    NAUG = FOX_BIAS_PIECES

    row = lax.broadcasted_iota(jnp.int32, (LANES, 2 * tq), 0)
    col = lax.broadcasted_iota(jnp.int32, (LANES, 2 * tq), 1)
    ones_rows = jnp.where((row < 2 * NAUG) & ((row >= NAUG) == (col >= tq)), 1.0, 0.0).astype(BF16)
    for pr in range(NP):
        qq_ref[pr] = jnp.concatenate(
            [jnp.concatenate([qT_ref[0, 2 * pr], qT_ref[0, 2 * pr + 1]], axis=1), ones_rows], axis=0)
    m_ref[...] = jnp.full(m_ref.shape, NEG, F32)
    l_ref[...] = jnp.zeros(l_ref.shape, F32)
    acc_ref[...] = jnp.zeros(acc_ref.shape, F32)
    za_ref[...] = jnp.full(za_ref.shape, NEG, F32)
    p_ref[...] = jnp.zeros(p_ref.shape, BF16)
    al_ref[...] = jnp.ones(al_ref.shape, F32)
    krow = lax.broadcasted_iota(jnp.int32, (tq, 2 * tq), 0)
    qcol = lax.broadcasted_iota(jnp.int32, (tq, 2 * tq), 1)
    causal = krow <= (qcol & (tq - 1))
    n_end = NP * (i + 1)

    def where_is(n):
        tile = jnp.clip(n >> 2, 0, i)
        return n & (NP - 1), tile, pl.multiple_of(tile * tq, tq)

    def stage_scores(n, j, guard):
        pr, _, r0 = where_is(n)
        kk = jnp.concatenate([k_ref[0, pr, pl.ds(r0, tq), :], aug_ref[0, pr, pl.ds(r0, tq), :]], axis=1)
        s = jnp.dot(kk, qq_ref[pr], preferred_element_type=F32)
        za_ref[j] = jnp.where(n < n_end, s, NEG) if guard else s

    def stage_softmax(n, j, masked):
        pr, _, _ = where_is(n)
        z = za_ref[j]
        if masked:
            z = jnp.where(causal, z, NEG)
        m = m_ref[pr]
        m_new = jnp.maximum(m, _col_reduce(z, jnp.max))
        p_ref[j] = jnp.exp2(z - m_new).astype(BF16)
        al_ref[j] = jnp.exp2(m - m_new)
        m_ref[pr] = m_new

    def stage_values(n, j):
        pr, tile, _ = where_is(n)
        pv = jnp.dot(vT_ref[0, tile, pr], p_ref[j], preferred_element_type=F32)
        al = al_ref[j]
        acc_ref[2 * pr] = al[:, :tq] * acc_ref[2 * pr] + pv[:HEAD_DIM, :tq]
        acc_ref[2 * pr + 1] = al[:, tq:] * acc_ref[2 * pr + 1] + pv[HEAD_DIM:2 * HEAD_DIM, tq:]
        l_ref[pr] = al * l_ref[pr] + pv[2 * HEAD_DIM:2 * HEAD_DIM + 1, :]

    def step(g, masked, guard):
        for j in range(G):
            stage_values((g - 2) * G + j, j)
        for j in range(G):
            stage_softmax((g - 1) * G + j, j, masked)
        for j in range(G):
            stage_scores(g * G + j, j, guard)

    def body_plain(g, carry):
        step(g, False, False)
        return carry

    def body_diag(g, carry):
        step(g, True, False)
        return carry

    g_diag = (NP // G) * i + 1
    g_end = (NP // G) * (i + 1)
    if NP // G == 2:
        def body_pair(k, carry):
            step(2 * k, False, False)
            step(2 * k + 1, False, False)
            return carry

        def body_quad(k, carry):
            body_pair(2 * k, carry)
            return body_pair(2 * k + 1, carry)

        lax.fori_loop(0, i // 2, body_quad, 0)
        lax.fori_loop(2 * (i // 2), i, body_pair, 0)
        step(2 * i, False, False)
        step(2 * i + 1, True, False)
    else:
        lax.fori_loop(0, g_diag, body_plain, 0)
        lax.fori_loop(g_diag, g_end, body_diag, 0)
    for j in range(G):
        stage_values((g_end - 2) * G + j, j)
    for j in range(G):
        stage_softmax((g_end - 1) * G + j, j, True)
    for j in range(G):
        stage_values((g_end - 1) * G + j, j)
    for pr in range(NP):
        la, lb = l_ref[pr][:, :tq], l_ref[pr][:, tq:]
        oT = jnp.concatenate([acc_ref[2 * pr] / la, acc_ref[2 * pr + 1] / lb], axis=0)
        o_ref[0, :, pr * LANES:(pr + 1) * LANES] = oT.T.astype(BF16)


def _fox_attention(fqT, fk, fvT, faug, *, tq):
    B, NP, S, _ = fk.shape
    nkv = S // tq
    vrows = fvT.shape[3]
    return pl.pallas_call(
        functools.partial(_fox_kernel, tq=tq),
        grid=(B, S // tq),
        in_specs=[
            pl.BlockSpec((1, FOX_HEADS, LANES, tq), lambda b, i: (b, 0, 0, i)),
            pl.BlockSpec((1, NP, S, LANES), lambda b, i: (b, 0, 0, 0)),
            pl.BlockSpec((1, nkv, NP, vrows, tq), lambda b, i: (b, 0, 0, 0, 0)),
            pl.BlockSpec((1, NP, S, LANES), lambda b, i: (b, 0, 0, 0)),
        ],
        out_specs=pl.BlockSpec((1, tq, MIX_WIDTH), lambda b, i: (b, i, 0)),
        out_shape=jax.ShapeDtypeStruct((B, S, MIX_WIDTH), BF16),
        scratch_shapes=[
            pltpu.VMEM((NP, 2 * LANES, 2 * tq), BF16),
            pltpu.VMEM((NP, 1, 2 * tq), F32),
            pltpu.VMEM((NP, 1, 2 * tq), F32),
            pltpu.VMEM((FOX_HEADS, HEAD_DIM, tq), F32),
            pltpu.VMEM((FOX_GROUP, tq, 2 * tq), F32),
            pltpu.VMEM((FOX_GROUP, tq, 2 * tq), BF16),
            pltpu.VMEM((FOX_GROUP, 1, 2 * tq), F32),
        ],
        compiler_params=pltpu.CompilerParams(dimension_semantics=("parallel", "arbitrary"),
                                             vmem_limit_bytes=VMEM_LIMIT_BYTES),
        name="fox_attention",
    )(fqT, fk, fvT, faug)


def _swa_kernel(sinks_ref, q_ref, kp_ref, kc_ref, vp_ref, vc_ref, o_ref):
    n = pl.program_id(1)
    L = BLOCK
    row = lax.broadcasted_iota(jnp.int32, (L, 2 * L), 0)
    col = lax.broadcasted_iota(jnp.int32, (L, 2 * L), 1)
    row_prev = row + jnp.where(n == 0, 2 * L, 0)
    mask = ((col < L) & (col > row_prev)) | ((col >= L) & ((col - L) <= row))
    lane = lax.broadcasted_iota(jnp.int32, (L, LANES), 1)
    low = lane < HEAD_DIM
    q = q_ref[0]
    zero = jnp.zeros((L, LANES), BF16)
    for pair in range(SWA_Q_HEADS // 2):
        g = pair // 2
        kf = jnp.concatenate([kp_ref[0, :, g * LANES:(g + 1) * LANES], kc_ref[0, :, g * LANES:(g + 1) * LANES]], axis=0)
        vf = jnp.concatenate([vp_ref[0, :, g * LANES:(g + 1) * LANES], vc_ref[0, :, g * LANES:(g + 1) * LANES]], axis=0)
        qp = q[:, pair * LANES:(pair + 1) * LANES]
        outs = []
        for half in range(2):
            sink = sinks_ref[2 * pair + half]
            qh = jnp.where(low if half == 0 else ~low, qp, zero)
            z = jnp.where(mask, _nt_dot(qh, kf), NEG)
            m = jnp.maximum(jnp.max(z, axis=1, keepdims=True), sink)
            pm = jnp.exp(z - m)
            den = jnp.sum(pm, axis=1, keepdims=True) + jnp.exp(sink - m)
            outs.append(jnp.dot(pm.astype(BF16), vf, preferred_element_type=F32) / den)
        o_ref[0, :, pair * LANES:(pair + 1) * LANES] = jnp.where(low, outs[0], outs[1]).astype(BF16)


def _swa_attention(sinks, sq, sk, sv):
    B, S, _ = sq.shape
    prev = lambda b, n: (b, jnp.maximum(n - 1, 0), 0)
    cur = lambda b, n: (b, n, 0)
    return pl.pallas_call(
        _swa_kernel,
        grid=(B, S // BLOCK),
        in_specs=[
            pl.BlockSpec(memory_space=pltpu.SMEM),
            pl.BlockSpec((1, BLOCK, 512), cur),
            pl.BlockSpec((1, BLOCK, 256), prev), pl.BlockSpec((1, BLOCK, 256), cur),
            pl.BlockSpec((1, BLOCK, 256), prev), pl.BlockSpec((1, BLOCK, 256), cur),
        ],
        out_specs=pl.BlockSpec((1, BLOCK, 512), cur),
        out_shape=jax.ShapeDtypeStruct((B, S, MIX_WIDTH), BF16),
        compiler_params=pltpu.CompilerParams(dimension_semantics=("parallel", "parallel"),
                                             vmem_limit_bytes=VMEM_LIMIT_BYTES),
        name="swa_attention",
    )(sinks, sq, sk, sk, sv, sv)


def _mlstm_kernel(qp_ref, q_ref, kp_ref, k_ref, v_ref, og_ref, gt_ref, gT_ref, sh_ref, cw_ref, cb_ref, on_ref,
                  y_ref, c_ref, n_ref, m_ref):
    c = pl.program_id(1)
    L = BLOCK

    @pl.when(c == 0)
    def _():
        c_ref[...] = jnp.zeros_like(c_ref)
        n_ref[...] = jnp.zeros_like(n_ref)
        m_ref[...] = jnp.zeros_like(m_ref)

    def conv_silu(p_ref, x_ref, off):
        cur = x_ref[0]
        prev = jnp.where(c > 0, p_ref[0], jnp.zeros_like(cur))
        sh = jnp.dot(sh_ref[...], jnp.concatenate([prev, cur], axis=0), preferred_element_type=F32)
        y = cb_ref[:, off:off + MIX_WIDTH]
        for j in range(CONV_WIDTH - 1):
            y = y + sh[j * L:(j + 1) * L] * cw_ref[j:j + 1, off:off + MIX_WIDTH]
        y = y + cur.astype(F32) * cw_ref[CONV_WIDTH - 1:CONV_WIDTH, off:off + MIX_WIDTH]
        return y * _sigmoid(y)

    qc = conv_silu(qp_ref, q_ref, 0)
    kc = conv_silu(kp_ref, k_ref, MIX_WIDTH) * (MLSTM_HEAD_DIM ** -0.5)
    gt = gt_ref[0]
    gT = gT_ref[0, 0]
    tri = lax.broadcasted_iota(jnp.int32, (L, L), 0) >= lax.broadcasted_iota(jnp.int32, (L, L), 1)

    for hd in range(MLSTM_HEADS):
        sl = slice(hd * MLSTM_HEAD_DIM, (hd + 1) * MLSTM_HEAD_DIM)
        q = qc[:, sl]
        k = kc[:, sl]
        qb, kb = q.astype(BF16), k.astype(BF16)
        v = v_ref[0, :, sl]
        i_col, b_col = gt[:, _GI + hd:_GI + hd + 1], gt[:, _GF + hd:_GF + hd + 1]
        i_row, b_row = gT[_GI + hd:_GI + hd + 1, :], gT[_GF + hd:_GF + hd + 1, :]
        b_last = b_col[L - 1:L, :]

        dmat = jnp.where(tri, b_col - b_row + i_row, NEG)
        m_loc = jnp.max(dmat, axis=1, keepdims=True)
        smat = jnp.exp(dmat - m_loc) * _nt_dot(qb, kb)
        s_v = jnp.dot(smat.astype(BF16), v, preferred_element_type=F32)
        s_sum = jnp.sum(smat, axis=1, keepdims=True)
        g_col = b_last - b_col + i_col
        g_max = jnp.max(b_last - b_row + i_row, axis=1, keepdims=True)
        kw = jnp.exp(g_col - g_max) * k
        kv_new = _tn_dot(kw.astype(BF16), v)
        k_new = jnp.sum(kw, axis=0, keepdims=True)

        m_prev = m_ref[hd]
        cmat = c_ref[hd]
        n_row = n_ref[hd]
        a_col = b_col + m_prev
        mt = jnp.maximum(a_col, m_loc)
        w_inter = jnp.exp(a_col - mt)
        w_intra = jnp.exp(m_loc - mt)
        num = w_inter * jnp.dot(qb, cmat.astype(BF16), preferred_element_type=F32) + w_intra * s_v
        den = w_inter * jnp.sum(q * n_row, axis=1, keepdims=True) + w_intra * s_sum
        hcur = num / jnp.maximum(jnp.abs(den), jnp.exp(-mt))

        m_new = jnp.maximum(b_last + m_prev, g_max)
        decay = jnp.exp(b_last + m_prev - m_new)
        w_new = jnp.exp(g_max - m_new)
        c_ref[hd] = decay * cmat + w_new * kv_new
        n_ref[hd] = decay * n_row + w_new * k_new
        m_ref[hd] = m_new

        hn = _rmsnorm_rows(hcur, on_ref[:, sl])
        y_ref[0, :, sl] = (hn * og_ref[0, :, sl].astype(F32)).astype(BF16)


def _mlstm(mq, mk, mv, og, gtok, gT, shifts, conv_w, conv_b, out_norm):
    B, S, _ = mq.shape
    cur = lambda b, c: (b, c, 0)
    prev = lambda b, c: (b, jnp.maximum(c - 1, 0), 0)
    tok = lambda w: pl.BlockSpec((1, BLOCK, w), cur)
    const = lambda shape: pl.BlockSpec(shape, lambda b, c: (0,) * len(shape))
    return pl.pallas_call(
        _mlstm_kernel,
        grid=(B, S // BLOCK),
        in_specs=[pl.BlockSpec((1, BLOCK, 512), prev), tok(512), pl.BlockSpec((1, BLOCK, 512), prev), tok(512),
                  tok(512), tok(512), tok(LANES),
                  pl.BlockSpec((1, 1, LANES, BLOCK), lambda b, c: (b, c, 0, 0)),
                  const(shifts.shape), const((CONV_WIDTH, 2 * MIX_WIDTH)), const((1, 2 * MIX_WIDTH)),
                  const((1, MIX_WIDTH))],
        out_specs=tok(512),
        out_shape=jax.ShapeDtypeStruct((B, S, MIX_WIDTH), BF16),
        scratch_shapes=[
            pltpu.VMEM((MLSTM_HEADS, MLSTM_HEAD_DIM, MLSTM_HEAD_DIM), F32),
            pltpu.VMEM((MLSTM_HEADS, 1, MLSTM_HEAD_DIM), F32),
            pltpu.VMEM((MLSTM_HEADS, 1, 1), F32),
        ],
        compiler_params=pltpu.CompilerParams(dimension_semantics=("parallel", "arbitrary"),
                                             vmem_limit_bytes=VMEM_LIMIT_BYTES),
        name="mlstm",
    )(mq, mq, mk, mk, mv, og, gtok, gT, shifts, conv_w, conv_b, out_norm)


def _merge_kernel(x_ref, g_ref, yf_ref, ys_ref, ym_ref, wgl_ref, wb_ref, wo_ref, o_ref):
    x = x_ref[...]
    h = _rmsnorm_rows(x, g_ref[...]).astype(BF16)
    merged = None
    for br, y_ref in enumerate((yf_ref, ys_ref, ym_ref)):
        gate = _sigmoid(jnp.dot(h, wgl_ref[:, br * D_MODEL:(br + 1) * D_MODEL], preferred_element_type=F32))
        term = gate * jnp.dot(y_ref[...], wb_ref[br], preferred_element_type=F32)
        merged = term if merged is None else merged + term
    o_ref[...] = x + jnp.dot(merged.astype(BF16), wo_ref[...], preferred_element_type=F32)


def _merge(x2, gain, yf, ys, ym, wgl, wb, wo, *, tm):
    T, D = x2.shape
    row = lambda w: pl.BlockSpec((tm, w), lambda t: (t, 0))
    const = lambda shape: pl.BlockSpec(shape, lambda t: (0,) * len(shape))
    return pl.pallas_call(
        _merge_kernel,
        grid=(T // tm,),
        in_specs=[row(D), const((1, D)), row(512), row(512), row(512),
                  const(wgl.shape), const(wb.shape), const(wo.shape)],
        out_specs=row(D),
        out_shape=jax.ShapeDtypeStruct((T, D), F32),
        compiler_params=pltpu.CompilerParams(dimension_semantics=("parallel",),
                                             vmem_limit_bytes=VMEM_LIMIT_BYTES),
        name="merge_out",
    )(x2, gain, yf, ys, ym, wgl, wb, wo)


def _mlp_kernel(x_ref, g_ref, wu_ref, wd_ref, o_ref, *, fc):
    x = x_ref[...]
    h = _rmsnorm_rows(x, g_ref[...]).astype(BF16)
    acc = x
    for f in range(D_FF // fc):
        a = jnp.maximum(jnp.dot(h, wu_ref[:, f * fc:(f + 1) * fc], preferred_element_type=F32), 0.0)
        acc = acc + jnp.dot((a * a).astype(BF16), wd_ref[f * fc:(f + 1) * fc, :], preferred_element_type=F32)
    o_ref[...] = acc


def _mlp(x2, gain, wu, wd, *, tm, fc):
    T, D = x2.shape
    row = pl.BlockSpec((tm, D), lambda t: (t, 0))
    const = lambda shape: pl.BlockSpec(shape, lambda t: (0,) * len(shape))
    return pl.pallas_call(
        functools.partial(_mlp_kernel, fc=fc),
        grid=(T // tm,),
        in_specs=[row, const((1, D)), const(wu.shape), const(wd.shape)],
        out_specs=row,
        out_shape=jax.ShapeDtypeStruct((T, D), F32),
        compiler_params=pltpu.CompilerParams(dimension_semantics=("parallel",),
                                             vmem_limit_bytes=VMEM_LIMIT_BYTES),
        name="mlp",
    )(x2, gain, wu, wd)


def _tiles(S):
    tm = 512 if S % 512 == 0 else S
    tq = 256 if S % 256 == 0 else S
    return tm, tq


def _layer_params(w_in, fox_f_bias, fox_q_norm, fox_k_norm, swa_q_norm, swa_k_norm, mlstm_i_bias, mlstm_f_bias):
    offs = np.cumsum(IN_SPLITS)[:-1].tolist()
    fq, fk, fv, ff, sq, sk, sv, mq, mk, mv, mi, mf, mo, gl = jnp.split(w_in, offs, axis=1)
    dup = lambda w: jnp.concatenate([w[:, :HEAD_DIM], w[:, :HEAD_DIM], w[:, HEAD_DIM:], w[:, HEAD_DIM:]], axis=1)

    def gate_group(fox, m_i, m_f):
        lead = fox.shape[:-1]
        fox8 = jnp.concatenate([fox.reshape(lead + (FOX_HEADS // 2, 2)),
                                jnp.zeros(lead + (FOX_HEADS // 2, 6), F32)], axis=-1).reshape(lead + (_GI,))
        return jnp.concatenate([fox8, m_i, m_f, jnp.zeros(lead + (LANES - _GF - MLSTM_HEADS,), F32)], axis=-1)

    gates = gate_group(ff, mi, mf)
    gbias = gate_group(fox_f_bias, mlstm_i_bias, mlstm_f_bias)[None, :]
    wt =jnp.concatenate([fk, sq, dup(sk), dup(sv), mq, mk, mv, mo, gates], axis=1).astype(BF16)
    wT = jnp.concatenate([fq, fv], axis=1).T.astype(BF16)
    fkg = jnp.tile(fox_k_norm * fox_q_norm * (HEAD_DIM ** -0.5 * LOG2E), FOX_HEADS)[None, :]
    sqg = jnp.tile(swa_q_norm * (HEAD_DIM ** -0.5), SWA_Q_HEADS)[None, :]
    skg = jnp.tile(swa_k_norm, 2 * SWA_KV_HEADS)[None, :]
    return wt, wT, fkg, sqg, skg, gbias, gl.astype(BF16)


def kernel(x, norm_mix, w_in, fox_f_bias, fox_q_norm, fox_k_norm, swa_q_norm, swa_k_norm, swa_sinks, conv_w,
           conv_b, mlstm_i_bias, mlstm_f_bias, mlstm_out_norm, w_branch, w_out, norm_mlp, w_up, w_down):
    B, S, D = x.shape
    depth = w_in.shape[0]
    tm, tq = _tiles(S)

    inv = ROPE_THETA ** (-jnp.arange(0, HEAD_DIM, 2, dtype=F32) / HEAD_DIM)
    ang = jnp.arange(S, dtype=F32)[:, None] * inv[None, :]
    cos, sin = jnp.cos(ang), jnp.sin(ang)
    cosf = jnp.tile(jnp.concatenate([cos, cos], axis=1), (1, LANES // HEAD_DIM))
    sinf = jnp.tile(jnp.concatenate([-sin, sin], axis=1), (1, LANES // HEAD_DIM))
    gm = jnp.asarray(np.kron(np.eye(MIX_WIDTH // HEAD_DIM), np.full((HEAD_DIM, HEAD_DIM), 1.0 / HEAD_DIM)), BF16)
    ltri = jnp.asarray(np.tril(np.ones((BLOCK, BLOCK))), BF16)
    shifts_np = np.zeros(((CONV_WIDTH - 1) * BLOCK, 2 * BLOCK))
    for j in range(CONV_WIDTH - 1):
        for t in range(BLOCK):
            shifts_np[j * BLOCK + t, BLOCK + t - (CONV_WIDTH - 1 - j)] = 1.0
    shifts = jnp.asarray(shifts_np, BF16)
    sel_np = np.zeros((FOX_BIAS_PIECES * LANES, MIX_WIDTH))
    for piece in range(FOX_BIAS_PIECES):
        for hd in range(FOX_HEADS):
            sel_np[piece * LANES + 8 * (hd // 2) + hd % 2, LANES * (hd // 2) + FOX_BIAS_PIECES * (hd % 2) + piece] = 1.0
    sel = jnp.asarray(sel_np, BF16)

    for l in range(depth):
        wt, wT, fkg, sqg, skg, gbias, wgl = _layer_params(
            w_in[l], fox_f_bias[l], fox_q_norm[l], fox_k_norm[l], swa_q_norm[l], swa_k_norm[l],
            mlstm_i_bias[l], mlstm_f_bias[l])
        (fqT, fvT, fk, sq, sk, sv, mq, mk, mv, og, gtok, gT, faug) = _in_proj(
            x, norm_mix[l][None, :], wt, wT, fkg, sqg, skg, gm, cosf, sinf, gbias, ltri, sel, tm=tm, tkv=tq)
        y_fox = _fox_attention(fqT, fk, fvT, faug, tq=tq)
        y_swa = _swa_attention(swa_sinks[l], sq, sk, sv)
        y_ml = _mlstm(mq, mk, mv, og, gtok, gT, shifts, conv_w[l], conv_b[l][None, :], mlstm_out_norm[l][None, :])
        x2 = _merge(x.reshape(B * S, D), norm_mix[l][None, :], y_fox.reshape(B * S, -1), y_swa.reshape(B * S, -1),
                    y_ml.reshape(B * S, -1), wgl, w_branch[l].astype(BF16), w_out[l].astype(BF16), tm=tm)
        x2 = _mlp(x2, norm_mlp[l][None, :], w_up[l].astype(BF16), w_down[l].astype(BF16), tm=tm, fc=1024)
        x = x2.reshape(B, S, D)
    return x
```

```python
import functools

import jax
import jax.numpy as jnp
import numpy as np
from jax import lax
from jax.experimental import pallas as pl
from jax.experimental.pallas import tpu as pltpu

F32 = jnp.float32
BF16 = jnp.bfloat16

D_MODEL = 1024
HEAD_DIM = 64
MIX_WIDTH = 512
FOX_HEADS = 8
SWA_Q_HEADS = 8
SWA_KV_HEADS = 2
SWA_WINDOW = 128
MLSTM_HEADS = 4
MLSTM_HEAD_DIM = 128
CONV_WIDTH = 4
D_FF = 4 * D_MODEL
N_BRANCHES = 3
BLOCK = 128
ROPE_THETA = 10000.0
EPS = 1e-6
NEG = -1e30
LOG2E = 1.4426950408889634
FOX_GROUP = 2

LANES = 128
FOX_VROWS = LANES + 16
FOX_BIAS_PIECES = 3
VMEM_LIMIT_BYTES = 56 * 1024 * 1024

_FK, _SQ, _SK, _SV, _MQ, _MK, _MV, _MO, _GT, _NTOK = 0, 512, 1024, 1280, 1536, 2048, 2560, 3072, 3584, 3712
_GI, _GF = 32, 36

IN_SPLITS = (512, 512, 512, 8, 512, 128, 128, 512, 512, 512, 4, 4, 512, 3 * D_MODEL)


def _nt_dot(a, b):
    return lax.dot_general(a, b, (((1,), (1,)), ((), ())), preferred_element_type=F32)


def _tn_dot(a, b):
    return lax.dot_general(a, b, (((0,), (0,)), ((), ())), preferred_element_type=F32)


def _sigmoid(x):
    return 1.0 / (1.0 + jnp.exp(-x))


def _log_sigmoid(x):
    return jnp.minimum(x, 0.0) - jnp.log(1.0 + jnp.exp(-jnp.abs(x)))


def _rmsnorm_rows(x, gain_row):
    ms = jnp.mean(x * x, axis=-1, keepdims=True)
    return x * lax.rsqrt(ms + EPS) * gain_row


def _in_proj_kernel(x_ref, g_ref, wt_ref, wT_ref, fkg_ref, sqg_ref, skg_ref, gm_ref, cos_ref, sin_ref,
                    gbias_ref, ltri_ref, sel_ref,
                    fqT_ref, fvT_ref, fk_ref, sq_ref, sk_ref, sv_ref, mq_ref, mk_ref, mv_ref, og_ref,
                    gtok_ref, gT_ref, aug_ref, carry_ref, *, tm, tkv):
    s = pl.program_id(1)
    h = _rmsnorm_rows(x_ref[0], g_ref[...]).astype(BF16)

    def proj(lo, hi):
        return jnp.dot(h, wt_ref[:, lo:hi], preferred_element_type=F32)

    def head_ms(u):
        u2 = (u * u).astype(BF16)
        gm = gm_ref[:2 * LANES, :2 * LANES]
        return jnp.concatenate([jnp.dot(u2[:, c:c + 2 * LANES], gm, preferred_element_type=F32)
                                for c in range(0, u.shape[1], 2 * LANES)], axis=1)

    def rope(u):
        lane = lax.broadcasted_iota(jnp.int32, (tm, LANES), 1)
        first = (lane % HEAD_DIM) < (HEAD_DIM // 2)
        c, sn = cos_ref[...], sin_ref[...]
        outs = []
        for j in range(u.shape[1] // LANES):
            uj = u[:, j * LANES:(j + 1) * LANES]
            partner = jnp.where(first, pltpu.roll(uj, LANES - HEAD_DIM // 2, 1), pltpu.roll(uj, HEAD_DIM // 2, 1))
            outs.append(uj * c + partner * sn)
        return jnp.concatenate(outs, axis=1)

    qT = _nt_dot(wT_ref[0:MIX_WIDTH, :], h)
    msT = jnp.dot(gm_ref[...], (qT * qT).astype(BF16), preferred_element_type=F32)
    qn = (qT * lax.rsqrt(msT + EPS)).astype(BF16)
    pad = jnp.zeros((HEAD_DIM, tm), BF16)
    for hd in range(FOX_HEADS):
        rows = qn[hd * HEAD_DIM:(hd + 1) * HEAD_DIM]
        fqT_ref[0, hd] = jnp.concatenate([rows, pad] if hd % 2 == 0 else [pad, rows], axis=0)
    vT = _nt_dot(wT_ref[MIX_WIDTH:2 * MIX_WIDTH, :], h).astype(BF16)
    ones = jnp.ones((FOX_VROWS - LANES, tkv), BF16)
    for c in range(tm // tkv):
        for pr in range(FOX_HEADS // 2):
            fvT_ref[0, c, pr] = jnp.concatenate([vT[pr * LANES:(pr + 1) * LANES, c * tkv:(c + 1) * tkv], ones], axis=0)

    u = proj(_FK, _SQ)
    kn = (u * lax.rsqrt(head_ms(u) + EPS) * fkg_ref[...]).astype(BF16)
    for pr in range(FOX_HEADS // 2):
        fk_ref[0, pr] = kn[:, pr * LANES:(pr + 1) * LANES]

    u = proj(_SQ, _SK)
    sq_ref[0] = rope(u * lax.rsqrt(head_ms(u) + EPS) * sqg_ref[...]).astype(BF16)
    u = proj(_SK, _SV)
    sk_ref[0] = rope(u * lax.rsqrt(head_ms(u) + EPS) * skg_ref[...]).astype(BF16)
    sv_ref[0] = proj(_SV, _MQ).astype(BF16)

    mq_ref[0] = proj(_MQ, _MK).astype(BF16)
    mk_ref[0] = proj(_MK, _MV).astype(BF16)
    mv_ref[0] = proj(_MV, _MO).astype(BF16)
    og_ref[0] = _sigmoid(proj(_MO, _GT)).astype(BF16)

    @pl.when(s == 0)
    def _():
        carry_ref[...] = jnp.zeros_like(carry_ref)

    g = proj(_GT, _NTOK) + gbias_ref[...]
    col = lax.broadcasted_iota(jnp.int32, (BLOCK, LANES), 1)
    is_i = (col >= _GI) & (col < _GF)
    is_fox = col < _GI
    def bf16_pieces(v):
        pieces = []
        for _ in range(FOX_BIAS_PIECES):
            part = v.astype(BF16)
            pieces.append(part)
            v = v - part.astype(F32)
        return pieces

    nchunk = tm // BLOCK
    val_pieces = bf16_pieces(_log_sigmoid(g))
    rhs = jnp.concatenate([piece[c * BLOCK:(c + 1) * BLOCK] for c in range(nchunk) for piece in val_pieces], axis=1)
    cs_all = jnp.dot(ltri_ref[...], rhs, preferred_element_type=F32)
    carry = carry_ref[...]
    outs = []
    for c in range(nchunk):
        blocks = [cs_all[:, (FOX_BIAS_PIECES * c + q) * LANES:(FOX_BIAS_PIECES * c + q + 1) * LANES]
                  for q in range(FOX_BIAS_PIECES)]
        cs = blocks[0] + blocks[1] + blocks[2] + jnp.where(is_fox, carry, 0.0)
        out_c = jnp.where(is_i, g[c * BLOCK:(c + 1) * BLOCK], cs)
        carry = out_c[BLOCK - 1:BLOCK, :]
        gtok_ref[0, c * BLOCK:(c + 1) * BLOCK, :] = out_c
        gT_ref[0, c] = out_c.T
        outs.append(out_c)
    carry_ref[...] = carry
    aug_pieces = bf16_pieces(jnp.concatenate(outs, axis=0) * (-LOG2E))
    aug = jnp.dot(jnp.concatenate(aug_pieces, axis=1), sel_ref[...], preferred_element_type=F32).astype(BF16)
    for pr in range(FOX_HEADS // 2):
        aug_ref[0, pr] = aug[:, pr * LANES:(pr + 1) * LANES]


def _in_proj(x, gain, wt, wT, fkg, sqg, skg, gm, cosf, sinf, gbias, ltri, sel, *, tm, tkv):
    B, S, D = x.shape
    nc = S // BLOCK
    const = lambda shape: pl.BlockSpec(shape, lambda b, s: (0,) * len(shape))
    tok = lambda w: pl.BlockSpec((1, tm, w), lambda b, s: (b, s, 0))
    out_shape = (
        jax.ShapeDtypeStruct((B, FOX_HEADS, LANES, S), BF16),
        jax.ShapeDtypeStruct((B, S // tkv, FOX_HEADS // 2, FOX_VROWS, tkv), BF16),
        jax.ShapeDtypeStruct((B, FOX_HEADS // 2, S, LANES), BF16),
        jax.ShapeDtypeStruct((B, S, 512), BF16),
        jax.ShapeDtypeStruct((B, S, 256), BF16),
        jax.ShapeDtypeStruct((B, S, 256), BF16),
        jax.ShapeDtypeStruct((B, S, 512), BF16),
        jax.ShapeDtypeStruct((B, S, 512), BF16),
        jax.ShapeDtypeStruct((B, S, 512), BF16),
        jax.ShapeDtypeStruct((B, S, 512), BF16),
        jax.ShapeDtypeStruct((B, S, LANES), F32),
        jax.ShapeDtypeStruct((B, nc, LANES, BLOCK), F32),
        jax.ShapeDtypeStruct((B, FOX_HEADS // 2, S, LANES), BF16),
    )
    out_specs = (
        pl.BlockSpec((1, FOX_HEADS, LANES, tm), lambda b, s: (b, 0, 0, s)),
        pl.BlockSpec((1, tm // tkv, FOX_HEADS // 2, FOX_VROWS, tkv), lambda b, s: (b, s, 0, 0, 0)),
        pl.BlockSpec((1, FOX_HEADS // 2, tm, LANES), lambda b, s: (b, 0, s, 0)),
        tok(512), tok(256), tok(256), tok(512), tok(512), tok(512), tok(512), tok(LANES),
        pl.BlockSpec((1, tm // BLOCK, LANES, BLOCK), lambda b, s: (b, s, 0, 0)),
        pl.BlockSpec((1, FOX_HEADS // 2, tm, LANES), lambda b, s: (b, 0, s, 0)),
    )
    in_specs = [
        pl.BlockSpec((1, tm, D), lambda b, s: (b, s, 0)),
        const((1, D)), const(wt.shape), const(wT.shape), const((1, 512)), const((1, 512)), const((1, 256)),
        const((512, 512)),
        pl.BlockSpec((tm, LANES), lambda b, s: (s, 0)), pl.BlockSpec((tm, LANES), lambda b, s: (s, 0)),
        const((1, LANES)), const((BLOCK, BLOCK)), const(sel.shape),
    ]
    return pl.pallas_call(
        functools.partial(_in_proj_kernel, tm=tm, tkv=tkv),
        grid=(B, S // tm), in_specs=in_specs, out_specs=out_specs, out_shape=out_shape,
        scratch_shapes=[pltpu.VMEM((1, LANES), F32)],
        compiler_params=pltpu.CompilerParams(dimension_semantics=("parallel", "arbitrary"),
                                             vmem_limit_bytes=VMEM_LIMIT_BYTES),
        name="in_proj",
    )(x, gain, wt, wT, fkg, sqg, skg, gm, cosf, sinf, gbias, ltri, sel)


def _col_reduce(x, op, ways=8):
    rows, cols = x.shape
    return op(op(x.reshape(ways, rows // ways, cols), axis=0), axis=0, keepdims=True)


def _fox_kernel(qT_ref, k_ref, vT_ref, aug_ref, o_ref, qq_ref, m_ref, l_ref, acc_ref, za_ref, p_ref, al_ref, *, tq):
    i = pl.program_id(1)
    NP = FOX_HEADS // 2
    G = FOX_GROUP
    NAUG = FOX_BIAS_PIECES

    row = lax.broadcasted_iota(jnp.int32, (LANES, 2 * tq), 0)
    col = lax.broadcasted_iota(jnp.int32, (LANES, 2 * tq), 1)
    ones_rows = jnp.where((row < 2 * NAUG) & ((row >= NAUG) == (col >= tq)), 1.0, 0.0).astype(BF16)
    for pr in range(NP):
        qq_ref[pr] = jnp.concatenate(
            [jnp.concatenate([qT_ref[0, 2 * pr], qT_ref[0, 2 * pr + 1]], axis=1), ones_rows], axis=0)
    m_ref[...] = jnp.full(m_ref.shape, NEG, F32)
    l_ref[...] = jnp.zeros(l_ref.shape, F32)
    acc_ref[...] = jnp.zeros(acc_ref.shape, F32)
    za_ref[...] = jnp.full(za_ref.shape, NEG, F32)
    p_ref[...] = jnp.zeros(p_ref.shape, BF16)
    al_ref[...] = jnp.ones(al_ref.shape, F32)
    krow = lax.broadcasted_iota(jnp.int32, (tq, 2 * tq), 0)
    qcol = lax.broadcasted_iota(jnp.int32, (tq, 2 * tq), 1)
    causal = krow <= (qcol & (tq - 1))
    n_end = NP * (i + 1)

    def where_is(n):
        tile = jnp.clip(n >> 2, 0, i)
        return n & (NP - 1), tile, pl.multiple_of(tile * tq, tq)

    def stage_scores(n, j, guard):
        pr, _, r0 = where_is(n)
        kk = jnp.concatenate([k_ref[0, pr, pl.ds(r0, tq), :], aug_ref[0, pr, pl.ds(r0, tq), :]], axis=1)
        s = jnp.dot(kk, qq_ref[pr], preferred_element_type=F32)
        za_ref[j] = jnp.where(n < n_end, s, NEG) if guard else s

    def stage_softmax(n, j, masked):
        pr, _, _ = where_is(n)
        z = za_ref[j]
        if masked:
            z = jnp.where(causal, z, NEG)
        m = m_ref[pr]
        m_new = jnp.maximum(m, _col_reduce(z, jnp.max))
        p_ref[j] = jnp.exp2(z - m_new).astype(BF16)
        al_ref[j] = jnp.exp2(m - m_new)
        m_ref[pr] = m_new

    def stage_values(n, j):
        pr, tile, _ = where_is(n)
        pv = jnp.dot(vT_ref[0, tile, pr], p_ref[j], preferred_element_type=F32)
        al = al_ref[j]
        acc_ref[2 * pr] = al[:, :tq] * acc_ref[2 * pr] + pv[:HEAD_DIM, :tq]
        acc_ref[2 * pr + 1] = al[:, tq:] * acc_ref[2 * pr + 1] + pv[HEAD_DIM:2 * HEAD_DIM, tq:]
        l_ref[pr] = al * l_ref[pr] + pv[2 * HEAD_DIM:2 * HEAD_DIM + 1, :]

    def step(g, masked, guard):
        for j in range(G):
            stage_values((g - 2) * G + j, j)
        for j in range(G):
            stage_softmax((g - 1) * G + j, j, masked)
        for j in range(G):
            stage_scores(g * G + j, j, guard)

    def body_plain(g, carry):
        step(g, False, False)
        return carry

    def body_diag(g, carry):
        step(g, True, False)
        return carry

    g_diag = (NP // G) * i + 1
    g_end = (NP // G) * (i + 1)
    if NP // G == 2:
        def body_pair(k, carry):
            step(2 * k, False, False)
            step(2 * k + 1, False, False)
            return carry

        def body_quad(k, carry):
            body_pair(2 * k, carry)
            return body_pair(2 * k + 1, carry)

        lax.fori_loop(0, i // 2, body_quad, 0)
        lax.fori_loop(2 * (i // 2), i, body_pair, 0)
        step(2 * i, False, False)
        step(2 * i + 1, True, False)
    else:
        lax.fori_loop(0, g_diag, body_plain, 0)
        lax.fori_loop(g_diag, g_end, body_diag, 0)
    for j in range(G):
        stage_values((g_end - 2) * G + j, j)
    for j in range(G):
        stage_softmax((g_end - 1) * G + j, j, True)
    for j in range(G):
        stage_values((g_end - 1) * G + j, j)
    for pr in range(NP):
        la, lb = l_ref[pr][:, :tq], l_ref[pr][:, tq:]
        oT = jnp.concatenate([acc_ref[2 * pr] / la, acc_ref[2 * pr + 1] / lb], axis=0)
        o_ref[0, :, pr * LANES:(pr + 1) * LANES] = oT.T.astype(BF16)


def _fox_attention(fqT, fk, fvT, faug, *, tq):
    B, NP, S, _ = fk.shape
    nkv = S // tq
    vrows = fvT.shape[3]
    return pl.pallas_call(
        functools.partial(_fox_kernel, tq=tq),
        grid=(B, S // tq),
        in_specs=[
            pl.BlockSpec((1, FOX_HEADS, LANES, tq), lambda b, i: (b, 0, 0, i)),
            pl.BlockSpec((1, NP, S, LANES), lambda b, i: (b, 0, 0, 0)),
            pl.BlockSpec((1, nkv, NP, vrows, tq), lambda b, i: (b, 0, 0, 0, 0)),
            pl.BlockSpec((1, NP, S, LANES), lambda b, i: (b, 0, 0, 0)),
        ],
        out_specs=pl.BlockSpec((1, tq, MIX_WIDTH), lambda b, i: (b, i, 0)),
        out_shape=jax.ShapeDtypeStruct((B, S, MIX_WIDTH), BF16),
        scratch_shapes=[
            pltpu.VMEM((NP, 2 * LANES, 2 * tq), BF16),
            pltpu.VMEM((NP, 1, 2 * tq), F32),
            pltpu.VMEM((NP, 1, 2 * tq), F32),
            pltpu.VMEM((FOX_HEADS, HEAD_DIM, tq), F32),
            pltpu.VMEM((FOX_GROUP, tq, 2 * tq), F32),
            pltpu.VMEM((FOX_GROUP, tq, 2 * tq), BF16),
            pltpu.VMEM((FOX_GROUP, 1, 2 * tq), F32),
        ],
        compiler_params=pltpu.CompilerParams(dimension_semantics=("parallel", "arbitrary"),
                                             vmem_limit_bytes=VMEM_LIMIT_BYTES),
        name="fox_attention",
    )(fqT, fk, fvT, faug)


def _swa_kernel(sinks_ref, q_ref, kp_ref, kc_ref, vp_ref, vc_ref, o_ref):
    n = pl.program_id(1)
    L = BLOCK
    row = lax.broadcasted_iota(jnp.int32, (L, 2 * L), 0)
    col = lax.broadcasted_iota(jnp.int32, (L, 2 * L), 1)
    row_prev = row + jnp.where(n == 0, 2 * L, 0)
    mask = ((col < L) & (col > row_prev)) | ((col >= L) & ((col - L) <= row))
    lane = lax.broadcasted_iota(jnp.int32, (L, LANES), 1)
    low = lane < HEAD_DIM
    q = q_ref[0]
    zero = jnp.zeros((L, LANES), BF16)
    for pair in range(SWA_Q_HEADS // 2):
        g = pair // 2
        kf = jnp.concatenate([kp_ref[0, :, g * LANES:(g + 1) * LANES], kc_ref[0, :, g * LANES:(g + 1) * LANES]], axis=0)
        vf = jnp.concatenate([vp_ref[0, :, g * LANES:(g + 1) * LANES], vc_ref[0, :, g * LANES:(g + 1) * LANES]], axis=0)
        qp = q[:, pair * LANES:(pair + 1) * LANES]
        outs = []
        for half in range(2):
            sink = sinks_ref[2 * pair + half]
            qh = jnp.where(low if half == 0 else ~low, qp, zero)
            z = jnp.where(mask, _nt_dot(qh, kf), NEG)
            m = jnp.maximum(jnp.max(z, axis=1, keepdims=True), sink)
            pm = jnp.exp(z - m)
            den = jnp.sum(pm, axis=1, keepdims=True) + jnp.exp(sink - m)
            outs.append(jnp.dot(pm.astype(BF16), vf, preferred_element_type=F32) / den)
        o_ref[0, :, pair * LANES:(pair + 1) * LANES] = jnp.where(low, outs[0], outs[1]).astype(BF16)


def _swa_attention(sinks, sq, sk, sv):
    B, S, _ = sq.shape
    prev = lambda b, n: (b, jnp.maximum(n - 1, 0), 0)
    cur = lambda b, n: (b, n, 0)
    return pl.pallas_call(
        _swa_kernel,
        grid=(B, S // BLOCK),
        in_specs=[
            pl.BlockSpec(memory_space=pltpu.SMEM),
            pl.BlockSpec((1, BLOCK, 512), cur),
            pl.BlockSpec((1, BLOCK, 256), prev), pl.BlockSpec((1, BLOCK, 256), cur),
            pl.BlockSpec((1, BLOCK, 256), prev), pl.BlockSpec((1, BLOCK, 256), cur),
        ],
        out_specs=pl.BlockSpec((1, BLOCK, 512), cur),
        out_shape=jax.ShapeDtypeStruct((B, S, MIX_WIDTH), BF16),
        compiler_params=pltpu.CompilerParams(dimension_semantics=("parallel", "parallel"),
                                             vmem_limit_bytes=VMEM_LIMIT_BYTES),
        name="swa_attention",
    )(sinks, sq, sk, sk, sv, sv)


def _mlstm_kernel(qp_ref, q_ref, kp_ref, k_ref, v_ref, og_ref, gt_ref, gT_ref, sh_ref, cw_ref, cb_ref, on_ref,
                  y_ref, c_ref, n_ref, m_ref):
    c = pl.program_id(1)
    L = BLOCK

    @pl.when(c == 0)
    def _():
        c_ref[...] = jnp.zeros_like(c_ref)
        n_ref[...] = jnp.zeros_like(n_ref)
        m_ref[...] = jnp.zeros_like(m_ref)

    def conv_silu(p_ref, x_ref, off):
        cur = x_ref[0]
        prev = jnp.where(c > 0, p_ref[0], jnp.zeros_like(cur))
        sh = jnp.dot(sh_ref[...], jnp.concatenate([prev, cur], axis=0), preferred_element_type=F32)
        y = cb_ref[:, off:off + MIX_WIDTH]
        for j in range(CONV_WIDTH - 1):
            y = y + sh[j * L:(j + 1) * L] * cw_ref[j:j + 1, off:off + MIX_WIDTH]
        y = y + cur.astype(F32) * cw_ref[CONV_WIDTH - 1:CONV_WIDTH, off:off + MIX_WIDTH]
        return y * _sigmoid(y)

    qc = conv_silu(qp_ref, q_ref, 0)
    kc = conv_silu(kp_ref, k_ref, MIX_WIDTH) * (MLSTM_HEAD_DIM ** -0.5)
    gt = gt_ref[0]
    gT = gT_ref[0, 0]
    tri = lax.broadcasted_iota(jnp.int32, (L, L), 0) >= lax.broadcasted_iota(jnp.int32, (L, L), 1)

    for hd in range(MLSTM_HEADS):
        sl = slice(hd * MLSTM_HEAD_DIM, (hd + 1) * MLSTM_HEAD_DIM)
        q = qc[:, sl]
        k = kc[:, sl]
        qb, kb = q.astype(BF16), k.astype(BF16)
        v = v_ref[0, :, sl]
        i_col, b_col = gt[:, _GI + hd:_GI + hd + 1], gt[:, _GF + hd:_GF + hd + 1]
        i_row, b_row = gT[_GI + hd:_GI + hd + 1, :], gT[_GF + hd:_GF + hd + 1, :]
        b_last = b_col[L - 1:L, :]

        dmat = jnp.where(tri, b_col - b_row + i_row, NEG)
        m_loc = jnp.max(dmat, axis=1, keepdims=True)
        smat = jnp.exp(dmat - m_loc) * _nt_dot(qb, kb)
        s_v = jnp.dot(smat.astype(BF16), v, preferred_element_type=F32)
        s_sum = jnp.sum(smat, axis=1, keepdims=True)
        g_col = b_last - b_col + i_col
        g_max = jnp.max(b_last - b_row + i_row, axis=1, keepdims=True)
        kw = jnp.exp(g_col - g_max) * k
        kv_new = _tn_dot(kw.astype(BF16), v)
        k_new = jnp.sum(kw, axis=0, keepdims=True)

        m_prev = m_ref[hd]
        cmat = c_ref[hd]
        n_row = n_ref[hd]
        a_col = b_col + m_prev
        mt = jnp.maximum(a_col, m_loc)
        w_inter = jnp.exp(a_col - mt)
        w_intra = jnp.exp(m_loc - mt)
        num = w_inter * jnp.dot(qb, cmat.astype(BF16), preferred_element_type=F32) + w_intra * s_v
        den = w_inter * jnp.sum(q * n_row, axis=1, keepdims=True) + w_intra * s_sum
        hcur = num / jnp.maximum(jnp.abs(den), jnp.exp(-mt))

        m_new = jnp.maximum(b_last + m_prev, g_max)
        decay = jnp.exp(b_last + m_prev - m_new)
        w_new = jnp.exp(g_max - m_new)
        c_ref[hd] = decay * cmat + w_new * kv_new
        n_ref[hd] = decay * n_row + w_new * k_new
        m_ref[hd] = m_new

        hn = _rmsnorm_rows(hcur, on_ref[:, sl])
        y_ref[0, :, sl] = (hn * og_ref[0, :, sl].astype(F32)).astype(BF16)


def _mlstm(mq, mk, mv, og, gtok, gT, shifts, conv_w, conv_b, out_norm):
    B, S, _ = mq.shape
    cur = lambda b, c: (b, c, 0)
    prev = lambda b, c: (b, jnp.maximum(c - 1, 0), 0)
    tok = lambda w: pl.BlockSpec((1, BLOCK, w), cur)
    const = lambda shape: pl.BlockSpec(shape, lambda b, c: (0,) * len(shape))
    return pl.pallas_call(
        _mlstm_kernel,
        grid=(B, S // BLOCK),
        in_specs=[pl.BlockSpec((1, BLOCK, 512), prev), tok(512), pl.BlockSpec((1, BLOCK, 512), prev), tok(512),
                  tok(512), tok(512), tok(LANES),
                  pl.BlockSpec((1, 1, LANES, BLOCK), lambda b, c: (b, c, 0, 0)),
                  const(shifts.shape), const((CONV_WIDTH, 2 * MIX_WIDTH)), const((1, 2 * MIX_WIDTH)),
                  const((1, MIX_WIDTH))],
        out_specs=tok(512),
        out_shape=jax.ShapeDtypeStruct((B, S, MIX_WIDTH), BF16),
        scratch_shapes=[
            pltpu.VMEM((MLSTM_HEADS, MLSTM_HEAD_DIM, MLSTM_HEAD_DIM), F32),
            pltpu.VMEM((MLSTM_HEADS, 1, MLSTM_HEAD_DIM), F32),
            pltpu.VMEM((MLSTM_HEADS, 1, 1), F32),
        ],
        compiler_params=pltpu.CompilerParams(dimension_semantics=("parallel", "arbitrary"),
                                             vmem_limit_bytes=VMEM_LIMIT_BYTES),
        name="mlstm",
    )(mq, mq, mk, mk, mv, og, gtok, gT, shifts, conv_w, conv_b, out_norm)


def _merge_kernel(x_ref, g_ref, yf_ref, ys_ref, ym_ref, wgl_ref, wb_ref, wo_ref, o_ref):
    x = x_ref[...]
    h = _rmsnorm_rows(x, g_ref[...]).astype(BF16)
    merged = None
    for br, y_ref in enumerate((yf_ref, ys_ref, ym_ref)):
        gate = _sigmoid(jnp.dot(h, wgl_ref[:, br * D_MODEL:(br + 1) * D_MODEL], preferred_element_type=F32))
        term = gate * jnp.dot(y_ref[...], wb_ref[br], preferred_element_type=F32)
        merged = term if merged is None else merged + term
    o_ref[...] = x + jnp.dot(merged.astype(BF16), wo_ref[...], preferred_element_type=F32)


def _merge(x2, gain, yf, ys, ym, wgl, wb, wo, *, tm):
    T, D = x2.shape
    row = lambda w: pl.BlockSpec((tm, w), lambda t: (t, 0))
    const = lambda shape: pl.BlockSpec(shape, lambda t: (0,) * len(shape))
    return pl.pallas_call(
        _merge_kernel,
        grid=(T // tm,),
        in_specs=[row(D), const((1, D)), row(512), row(512), row(512),
                  const(wgl.shape), const(wb.shape), const(wo.shape)],
        out_specs=row(D),
        out_shape=jax.ShapeDtypeStruct((T, D), F32),
        compiler_params=pltpu.CompilerParams(dimension_semantics=("parallel",),
                                             vmem_limit_bytes=VMEM_LIMIT_BYTES),
        name="merge_out",
    )(x2, gain, yf, ys, ym, wgl, wb, wo)


def _mlp_kernel(x_ref, g_ref, wu_ref, wd_ref, o_ref, *, fc):
    x = x_ref[...]
    h = _rmsnorm_rows(x, g_ref[...]).astype(BF16)
    acc = x
    for f in range(D_FF // fc):
        a = jnp.maximum(jnp.dot(h, wu_ref[:, f * fc:(f + 1) * fc], preferred_element_type=F32), 0.0)
        acc = acc + jnp.dot((a * a).astype(BF16), wd_ref[f * fc:(f + 1) * fc, :], preferred_element_type=F32)
    o_ref[...] = acc


def _mlp(x2, gain, wu, wd, *, tm, fc):
    T, D = x2.shape
    row = pl.BlockSpec((tm, D), lambda t: (t, 0))
    const = lambda shape: pl.BlockSpec(shape, lambda t: (0,) * len(shape))
    return pl.pallas_call(
        functools.partial(_mlp_kernel, fc=fc),
        grid=(T // tm,),
        in_specs=[row, const((1, D)), const(wu.shape), const(wd.shape)],
        out_specs=row,
        out_shape=jax.ShapeDtypeStruct((T, D), F32),
        compiler_params=pltpu.CompilerParams(dimension_semantics=("parallel",),
                                             vmem_limit_bytes=VMEM_LIMIT_BYTES),
        name="mlp",
    )(x2, gain, wu, wd)


def _tiles(S):
    tm = 512 if S % 512 == 0 else S
    tq = 256 if S % 256 == 0 else S
    return tm, tq


def _layer_params(w_in, fox_f_bias, fox_q_norm, fox_k_norm, swa_q_norm, swa_k_norm, mlstm_i_bias, mlstm_f_bias):
    offs = np.cumsum(IN_SPLITS)[:-1].tolist()
    fq, fk, fv, ff, sq, sk, sv, mq, mk, mv, mi, mf, mo, gl = jnp.split(w_in, offs, axis=1)
    dup = lambda w: jnp.concatenate([w[:, :HEAD_DIM], w[:, :HEAD_DIM], w[:, HEAD_DIM:], w[:, HEAD_DIM:]], axis=1)

    def gate_group(fox, m_i, m_f):
        lead = fox.shape[:-1]
        fox8 = jnp.concatenate([fox.reshape(lead + (FOX_HEADS // 2, 2)),
                                jnp.zeros(lead + (FOX_HEADS // 2, 6), F32)], axis=-1).reshape(lead + (_GI,))
        return jnp.concatenate([fox8, m_i, m_f, jnp.zeros(lead + (LANES - _GF - MLSTM_HEADS,), F32)], axis=-1)

    gates = gate_group(ff, mi, mf)
    gbias = gate_group(fox_f_bias, mlstm_i_bias, mlstm_f_bias)[None, :]
    wt =jnp.concatenate([fk, sq, dup(sk), dup(sv), mq, mk, mv, mo, gates], axis=1).astype(BF16)
    wT = jnp.concatenate([fq, fv], axis=1).T.astype(BF16)
    fkg = jnp.tile(fox_k_norm * fox_q_norm * (HEAD_DIM ** -0.5 * LOG2E), FOX_HEADS)[None, :]
    sqg = jnp.tile(swa_q_norm * (HEAD_DIM ** -0.5), SWA_Q_HEADS)[None, :]
    skg = jnp.tile(swa_k_norm, 2 * SWA_KV_HEADS)[None, :]
    return wt, wT, fkg, sqg, skg, gbias, gl.astype(BF16)


def kernel(x, norm_mix, w_in, fox_f_bias, fox_q_norm, fox_k_norm, swa_q_norm, swa_k_norm, swa_sinks, conv_w,
           conv_b, mlstm_i_bias, mlstm_f_bias, mlstm_out_norm, w_branch, w_out, norm_mlp, w_up, w_down):
    B, S, D = x.shape
    depth = w_in.shape[0]
    tm, tq = _tiles(S)

    inv = ROPE_THETA ** (-jnp.arange(0, HEAD_DIM, 2, dtype=F32) / HEAD_DIM)
    ang = jnp.arange(S, dtype=F32)[:, None] * inv[None, :]
    cos, sin = jnp.cos(ang), jnp.sin(ang)
    cosf = jnp.tile(jnp.concatenate([cos, cos], axis=1), (1, LANES // HEAD_DIM))
    sinf = jnp.tile(jnp.concatenate([-sin, sin], axis=1), (1, LANES // HEAD_DIM))
    gm = jnp.asarray(np.kron(np.eye(MIX_WIDTH // HEAD_DIM), np.full((HEAD_DIM, HEAD_DIM), 1.0 / HEAD_DIM)), BF16)
    ltri = jnp.asarray(np.tril(np.ones((BLOCK, BLOCK))), BF16)
    shifts_np = np.zeros(((CONV_WIDTH - 1) * BLOCK, 2 * BLOCK))
    for j in range(CONV_WIDTH - 1):
        for t in range(BLOCK):
            shifts_np[j * BLOCK + t, BLOCK + t - (CONV_WIDTH - 1 - j)] = 1.0
    shifts = jnp.asarray(shifts_np, BF16)
    sel_np = np.zeros((FOX_BIAS_PIECES * LANES, MIX_WIDTH))
    for piece in range(FOX_BIAS_PIECES):
        for hd in range(FOX_HEADS):
            sel_np[piece * LANES + 8 * (hd // 2) + hd % 2, LANES * (hd // 2) + FOX_BIAS_PIECES * (hd % 2) + piece] = 1.0
    sel = jnp.asarray(sel_np, BF16)

    for l in range(depth):
        wt, wT, fkg, sqg, skg, gbias, wgl = _layer_params(
            w_in[l], fox_f_bias[l], fox_q_norm[l], fox_k_norm[l], swa_q_norm[l], swa_k_norm[l],
            mlstm_i_bias[l], mlstm_f_bias[l])
        (fqT, fvT, fk, sq, sk, sv, mq, mk, mv, og, gtok, gT, faug) = _in_proj(
            x, norm_mix[l][None, :], wt, wT, fkg, sqg, skg, gm, cosf, sinf, gbias, ltri, sel, tm=tm, tkv=tq)
        y_fox = _fox_attention(fqT, fk, fvT, faug, tq=tq)
        y_swa = _swa_attention(swa_sinks[l], sq, sk, sv)
        y_ml = _mlstm(mq, mk, mv, og, gtok, gT, shifts, conv_w[l], conv_b[l][None, :], mlstm_out_norm[l][None, :])
        x2 = _merge(x.reshape(B * S, D), norm_mix[l][None, :], y_fox.reshape(B * S, -1), y_swa.reshape(B * S, -1),
                    y_ml.reshape(B * S, -1), wgl, w_branch[l].astype(BF16), w_out[l].astype(BF16), tm=tm)
        x2 = _mlp(x2, norm_mlp[l][None, :], w_up[l].astype(BF16), w_down[l].astype(BF16), tm=tm, fc=1024)
        x = x2.reshape(B, S, D)
    return x
```
